```python
import jax, jax.numpy as jnp
from jax import lax
import numpy as np

D_MODEL = 1024
BATCH = 2
SEQ = 8192
DEPTH = 4

SWA_HEADS = 8
SWA_KV_HEADS = 2
SWA_HEAD_DIM = 64
SWA_WINDOW = 128
SWA_BLOCK = 128
ROPE_THETA = 10000.0
GLA_HEADS = 4
GLA_DK = D_MODEL // 2 // GLA_HEADS
GLA_DV = D_MODEL // GLA_HEADS
GLA_GATE_RANK = 16
GLA_TAU = 16.0
GLA_CHUNK = 64
D_FF = 2816
EPS = 1e-6

SWA_Q_W = SWA_HEADS * SWA_HEAD_DIM
SWA_KV_W = SWA_KV_HEADS * SWA_HEAD_DIM
GLA_K_W = GLA_HEADS * GLA_DK
GLA_V_W = GLA_HEADS * GLA_DV
IN_SPLITS = (SWA_Q_W, SWA_KV_W, SWA_KV_W, GLA_K_W, GLA_K_W, GLA_V_W, GLA_V_W, GLA_GATE_RANK, D_MODEL, D_MODEL)
IN_COLS = sum(IN_SPLITS)

kernel_name = "hybrid_swa_sink_gla_macaron"


def rmsnorm(x, g):
    xf = x.astype(jnp.float32)
    y = xf * lax.rsqrt(jnp.mean(xf * xf, axis=-1, keepdims=True) + EPS)
    return (y * g.astype(jnp.float32)).astype(x.dtype)


def swiglu(h, w_gate, w_up, w_down):
    return (jax.nn.silu(h @ w_gate) * (h @ w_up)) @ w_down


def split_cols(z, sizes):
    idx = np.cumsum(np.array(sizes))[:-1].tolist()
    return jnp.split(z, idx, axis=-1)


def rope_tables(T):
    inv_freq = ROPE_THETA ** (-jnp.arange(0, SWA_HEAD_DIM, 2, dtype=jnp.float32) / SWA_HEAD_DIM)
    ang = jnp.arange(T, dtype=jnp.float32)[:, None] * inv_freq[None, :]
    return jnp.cos(ang), jnp.sin(ang)


def apply_rope(x, cos, sin):
    x1, x2 = jnp.split(x, 2, axis=-1)
    c = cos[None, :, None, :]
    s = sin[None, :, None, :]
    return jnp.concatenate([x1 * c - x2 * s, x2 * c + x1 * s], axis=-1)


def swa_attention(q, k, v, q_gain, k_gain, sinks, cos, sin):
    B, T = q.shape[0], q.shape[1]
    G = SWA_HEADS // SWA_KV_HEADS
    nb = T // SWA_BLOCK
    q = apply_rope(rmsnorm(q, q_gain).astype(jnp.float32), cos, sin)
    k = apply_rope(rmsnorm(k, k_gain).astype(jnp.float32), cos, sin)
    v = v.astype(jnp.float32)
    q = q.reshape(B, nb, SWA_BLOCK, SWA_KV_HEADS, G, SWA_HEAD_DIM)
    k = k.reshape(B, nb, SWA_BLOCK, SWA_KV_HEADS, SWA_HEAD_DIM)
    v = v.reshape(B, nb, SWA_BLOCK, SWA_KV_HEADS, SWA_HEAD_DIM)

    def band(t):
        prev = jnp.pad(t[:, :-1], ((0, 0), (1, 0), (0, 0), (0, 0), (0, 0)))
        return jnp.concatenate([prev, t], axis=2)

    kb, vb = band(k), band(v)
    s = jnp.einsum('bnqhgd,bnkhd->bnhgqk', q, kb) * (SWA_HEAD_DIM ** -0.5)
    i = jnp.arange(SWA_BLOCK)[:, None]
    j = jnp.arange(2 * SWA_BLOCK)[None, :]
    rel = i + SWA_BLOCK - j
    blk = jnp.arange(nb)[:, None, None]
    valid = (rel >= 0) & (rel < SWA_WINDOW) & ((blk > 0) | (j >= SWA_BLOCK))
    s = jnp.where(valid[None, :, None, None], s, -jnp.inf)
    sink = sinks.astype(jnp.float32).reshape(1, 1, SWA_KV_HEADS, G, 1, 1)
    m = jnp.maximum(jnp.max(s, axis=-1, keepdims=True), sink)
    p = jnp.exp(s - m)
    denom = jnp.sum(p, axis=-1, keepdims=True) + jnp.exp(sink - m)
    o = jnp.einsum('bnhgqk,bnkhd->bnqhgd', p / denom, vb)
    return o.reshape(B, T, SWA_Q_W)


def gla_attention(q, k, v, log_a):
    B, T = q.shape[0], q.shape[1]
    nc = T // GLA_CHUNK

    def chunks(t, d):
        return t.astype(jnp.float32).reshape(B, nc, GLA_CHUNK, GLA_HEADS, d).transpose(1, 0, 3, 2, 4)

    qc = chunks(q, GLA_DK) * (GLA_DK ** -0.5)
    kc = chunks(k, GLA_DK)
    vc = chunks(v, GLA_DV)
    gc = chunks(log_a, GLA_DK)
    causal = jnp.tril(jnp.ones((GLA_CHUNK, GLA_CHUNK), dtype=bool))[:, :, None]

    def step(S, inp):
        qi, ki, vi, gi = inp
        b = jnp.cumsum(gi, axis=-2)
        b_last = b[:, :, -1:, :]
        o_inter = jnp.einsum('bhtc,bhcv->bhtv', qi * jnp.exp(b), S)
        diff = b[:, :, :, None, :] - b[:, :, None, :, :]
        decay = jnp.exp(jnp.where(causal, diff, -jnp.inf))
        attn = jnp.einsum('bhtsc,bhsc->bhts', qi[:, :, :, None, :] * decay, ki)
        o_intra = jnp.einsum('bhts,bhsv->bhtv', attn, vi)
        S = jnp.exp(b_last[:, :, 0, :])[..., None] * S + jnp.einsum('bhsc,bhsv->bhcv', ki * jnp.exp(b_last - b), vi)
        return S, o_inter + o_intra

    S0 = jnp.zeros((B, GLA_HEADS, GLA_DK, GLA_DV), jnp.float32)
    _, o = lax.scan(step, S0, (qc, kc, vc, gc))
    return o.transpose(1, 0, 3, 2, 4).reshape(B, T, GLA_HEADS, GLA_DV)


def setup_inputs(seed: int = 0) -> dict:
    key = jax.random.key(seed)
    ks = jax.random.split(key, 24)
    L, D = DEPTH, D_MODEL

    def w(k, shape, fan_in):
        return jax.random.normal(k, shape, jnp.float32) * (fan_in ** -0.5)

    def gain(k, shape):
        return 1.0 + 0.02 * jax.random.normal(k, shape, jnp.float32)

    return {
        "x": jax.random.normal(ks[0], (BATCH, SEQ, D), jnp.float32),
        "ffn1_norm": gain(ks[1], (L, D)),
        "ffn1_w_gate": w(ks[2], (L, D, D_FF), D),
        "ffn1_w_up": w(ks[3], (L, D, D_FF), D),
        "ffn1_w_down": w(ks[4], (L, D_FF, D), D_FF),
        "mix_norm": gain(ks[5], (L, D)),
        "w_in": w(ks[6], (L, D, IN_COLS), D),
        "swa_q_norm": gain(ks[7], (L, SWA_HEAD_DIM)),
        "swa_k_norm": gain(ks[8], (L, SWA_HEAD_DIM)),
        "swa_sinks": 0.5 * jax.random.normal(ks[9], (L, SWA_HEADS), jnp.float32),
        "gla_w_gate": w(ks[10], (L, GLA_GATE_RANK, GLA_K_W), GLA_GATE_RANK),
        "gla_gate_bias": 0.1 * jax.random.normal(ks[11], (L, GLA_K_W), jnp.float32),
        "gla_out_norm": gain(ks[12], (L, GLA_V_W)),
        "w_proj_a": w(ks[13], (L, SWA_Q_W, D), SWA_Q_W),
        "w_proj_b": w(ks[14], (L, GLA_V_W, D), GLA_V_W),
        "w_out": w(ks[15], (L, D, D), D),
        "ffn2_norm": gain(ks[16], (L, D)),
        "ffn2_w_gate": w(ks[17], (L, D, D_FF), D),
        "ffn2_w_up": w(ks[18], (L, D, D_FF), D),
        "ffn2_w_down": w(ks[19], (L, D_FF, D), D_FF),
    }


def reference(x, ffn1_norm, ffn1_w_gate, ffn1_w_up, ffn1_w_down, mix_norm, w_in,
              swa_q_norm, swa_k_norm, swa_sinks, gla_w_gate, gla_gate_bias, gla_out_norm,
              w_proj_a, w_proj_b, w_out, ffn2_norm, ffn2_w_gate, ffn2_w_up, ffn2_w_down):
    B, T = x.shape[0], x.shape[1]
    cos, sin = rope_tables(T)
    for l in range(DEPTH):
        h = rmsnorm(x, ffn1_norm[l])
        x = x + 0.5 * swiglu(h, ffn1_w_gate[l], ffn1_w_up[l], ffn1_w_down[l])

        h = rmsnorm(x, mix_norm[l])
        z = h @ w_in[l]
        q_a, k_a, v_a, q_b, k_b, v_b, r_b, g_lr, gate_a, gate_b = split_cols(z, IN_SPLITS)

        o_a = swa_attention(q_a.reshape(B, T, SWA_HEADS, SWA_HEAD_DIM),
                            k_a.reshape(B, T, SWA_KV_HEADS, SWA_HEAD_DIM),
                            v_a.reshape(B, T, SWA_KV_HEADS, SWA_HEAD_DIM),
                            swa_q_norm[l], swa_k_norm[l], swa_sinks[l], cos, sin).astype(x.dtype)

        log_a = jax.nn.log_sigmoid((g_lr @ gla_w_gate[l] + gla_gate_bias[l]).astype(jnp.float32)) / GLA_TAU
        o_b = gla_attention(q_b.reshape(B, T, GLA_HEADS, GLA_DK),
                            k_b.reshape(B, T, GLA_HEADS, GLA_DK),
                            v_b.reshape(B, T, GLA_HEADS, GLA_DV),
                            log_a.reshape(B, T, GLA_HEADS, GLA_DK))
        o_b = rmsnorm(o_b, gla_out_norm[l].reshape(GLA_HEADS, GLA_DV)).reshape(B, T, GLA_V_W).astype(x.dtype)
        o_b = o_b * jax.nn.silu(r_b)

        y_a = o_a @ w_proj_a[l]
        y_b = o_b @ w_proj_b[l]
        merged = jax.nn.sigmoid(gate_a) * y_a + jax.nn.sigmoid(gate_b) * y_b
        x = x + merged @ w_out[l]

        h = rmsnorm(x, ffn2_norm[l])
        x = x + 0.5 * swiglu(h, ffn2_w_gate[l], ffn2_w_up[l], ffn2_w_down[l])
    return x
```

```python
import functools

import jax
import jax.numpy as jnp
from jax import lax
from jax.experimental import pallas as pl
from jax.experimental.pallas import tpu as pltpu

F32 = jnp.float32
BF16 = jnp.bfloat16

D_MODEL = 1024
DEPTH = 4
D_FF = 2816
EPS = 1e-6

SWA_HEADS = 8
SWA_KV_HEADS = 2
SWA_GROUP = SWA_HEADS // SWA_KV_HEADS
SWA_HEAD_DIM = 64
SWA_BLOCK = 128
ROPE_THETA = 10000.0
SWA_Q_W = SWA_HEADS * SWA_HEAD_DIM
SWA_KV_W = SWA_KV_HEADS * SWA_HEAD_DIM

GLA_HEADS = 4
GLA_DK = 128
GLA_DV = 256
GLA_GATE_RANK = 16
GLA_TAU = 16.0
GLA_K_W = GLA_HEADS * GLA_DK
GLA_V_W = GLA_HEADS * GLA_DV

IN_SPLITS = (SWA_Q_W, SWA_KV_W, SWA_KV_W, GLA_K_W, GLA_K_W, GLA_V_W, GLA_V_W, GLA_GATE_RANK, D_MODEL, D_MODEL)
IN_COLS = sum(IN_SPLITS)

LANES = 128
GLR_PAD = LANES
PROJ_W = (SWA_Q_W, SWA_KV_W, SWA_KV_W, GLA_K_W, GLA_K_W, GLA_V_W, GLA_V_W, D_MODEL, D_MODEL, GLR_PAD)
PROJ_COLS = sum(PROJ_W)

ROW_TILE = 512
SWA_TQ = 512
GLA_CHUNK = 128
GLA_TC = 512
VMEM_LIMIT = 56 * 1024 * 1024


def _rms(x, gain):
    ms = jnp.mean(x * x, axis=-1, keepdims=True)
    return x * lax.rsqrt(ms + EPS) * gain


def _dot(a, b):
    return jnp.dot(a, b, preferred_element_type=F32)


def _dot_nt(a, b):
    return lax.dot_general(a, b, (((1,), (1,)), ((), ())), preferred_element_type=F32)


def _dot_tn(a, b):
    return lax.dot_general(a, b, (((0,), (0,)), ((), ())), preferred_element_type=F32)


def _split_dot(a_bf16, x):
    hi = x.astype(BF16)
    lo = (x - hi.astype(F32)).astype(BF16)
    return _dot(a_bf16, hi) + _dot(a_bf16, lo)


def _resident(shape, index_map):
    return pl.BlockSpec(shape, index_map, pipeline_mode=pl.Buffered(1))


def _ffn_body(x_ref, gain_ref, wg_ref, wu_ref, wd_ref, o_ref):
    x = x_ref[...]
    h = _rms(x, gain_ref[...]).astype(BF16)
    g = _dot(h, wg_ref[...])
    u = _dot(h, wu_ref[...])
    a = (g * jax.nn.sigmoid(g) * u).astype(BF16)
    o_ref[...] = x + 0.5 * _dot(a, wd_ref[...])


def _ffn(x, gain, wg, wu, wd, layer):
    n = x.shape[0]
    return pl.pallas_call(
        _ffn_body,
        grid=(n // ROW_TILE,),
        in_specs=[
            pl.BlockSpec((ROW_TILE, D_MODEL), lambda i: (i, 0)),
            _resident((None, 1, D_MODEL), lambda i: (layer, 0, 0)),
            _resident((None, D_MODEL, D_FF), lambda i: (layer, 0, 0)),
            _resident((None, D_MODEL, D_FF), lambda i: (layer, 0, 0)),
            _resident((None, D_FF, D_MODEL), lambda i: (layer, 0, 0)),
        ],
        out_specs=pl.BlockSpec((ROW_TILE, D_MODEL), lambda i: (i, 0)),
        out_shape=jax.ShapeDtypeStruct((n, D_MODEL), F32),
        compiler_params=pltpu.CompilerParams(
            dimension_semantics=("arbitrary",), vmem_limit_bytes=VMEM_LIMIT),
        name="ffn",
    )(x, gain, wg, wu, wd)


def _inproj_body(x_ref, gain_ref, w_ref, *o_refs):
    h = _rms(x_ref[...], gain_ref[...]).astype(BF16)
    off = 0
    for o_ref, width in zip(o_refs, PROJ_W):
        o_ref[...] = _dot(h, w_ref[:, off:off + width]).astype(o_ref.dtype)
        off += width


def _inproj(x, gain, w, layer):
    n = x.shape[0]
    return pl.pallas_call(
        _inproj_body,
        grid=(n // ROW_TILE,),
        in_specs=[
            pl.BlockSpec((ROW_TILE, D_MODEL), lambda i: (i, 0)),
            _resident((None, 1, D_MODEL), lambda i: (layer, 0, 0)),
            _resident((None, D_MODEL, PROJ_COLS), lambda i: (layer, 0, 0)),
        ],
        out_specs=[pl.BlockSpec((ROW_TILE, width), lambda i: (i, 0)) for width in PROJ_W],
        out_shape=[jax.ShapeDtypeStruct((n, width), BF16) for width in PROJ_W],
        compiler_params=pltpu.CompilerParams(
            dimension_semantics=("arbitrary",), vmem_limit_bytes=VMEM_LIMIT),
        name="inproj",
    )(x, gain, w)


def _head_norm_rope(x, gain, cos, sin_signed, ones_blockdiag, lane):
    ssq = _split_dot_right(x * x, ones_blockdiag)
    xn = x * lax.rsqrt(ssq * (1.0 / SWA_HEAD_DIM) + EPS) * gain
    half = SWA_HEAD_DIM // 2
    partner = jnp.where((lane % SWA_HEAD_DIM) < half,
                        pltpu.roll(xn, LANES - half, 1), pltpu.roll(xn, half, 1))
    return xn * cos + partner * sin_signed


def _split_dot_right(x, b_bf16):
    hi = x.astype(BF16)
    lo = (x - hi.astype(F32)).astype(BF16)
    return _dot(hi, b_bf16) + _dot(lo, b_bf16)


def _swa_body(sinks_ref, q_ref, k_ref, v_ref, cos_ref, sin_ref, qg_ref, kg_ref, o_ref, kbuf, vbuf):
    i = pl.program_id(1)
    tq = q_ref.shape[0]
    nblk = tq // SWA_BLOCK
    lane = lax.broadcasted_iota(jnp.int32, (1, LANES), 1)
    first_half = lane < SWA_HEAD_DIM
    r_i = lax.broadcasted_iota(jnp.int32, (LANES, LANES), 0) // SWA_HEAD_DIM
    c_i = lax.broadcasted_iota(jnp.int32, (LANES, LANES), 1) // SWA_HEAD_DIM
    ones_bd = (r_i == c_i).astype(BF16)

    @pl.when(i == 0)
    def _():
        kbuf[0:SWA_BLOCK, :] = jnp.zeros((SWA_BLOCK, LANES), BF16)
        vbuf[0:SWA_BLOCK, :] = jnp.zeros((SWA_BLOCK, LANES), BF16)

    cos = cos_ref[...]
    sin = sin_ref[...]
    k = _head_norm_rope(k_ref[...].astype(F32), kg_ref[...], cos, sin, ones_bd, lane)
    kbuf[SWA_BLOCK:, :] = k.astype(BF16)
    vbuf[SWA_BLOCK:, :] = v_ref[...]

    scale = SWA_HEAD_DIM ** -0.5
    q_cols = []
    for c in range(SWA_Q_W // LANES):
        qc = _head_norm_rope(q_ref[:, c * LANES:(c + 1) * LANES].astype(F32),
                             qg_ref[...], cos, sin, ones_bd, lane) * scale
        q_cols.append(qc)

    kall = kbuf[...]
    vall = vbuf[...]
    kdup = [jnp.where(first_half, kall, pltpu.roll(kall, SWA_HEAD_DIM, 1)),
            jnp.where(first_half, pltpu.roll(kall, SWA_HEAD_DIM, 1), kall)]
    vdup = [jnp.where(first_half, vall, pltpu.roll(vall, SWA_HEAD_DIM, 1)),
            jnp.where(first_half, pltpu.roll(vall, SWA_HEAD_DIM, 1), vall)]

    rows = SWA_GROUP * SWA_BLOCK
    t_idx = lax.broadcasted_iota(jnp.int32, (rows, 2 * SWA_BLOCK), 0) % SWA_BLOCK
    j_idx = lax.broadcasted_iota(jnp.int32, (rows, 2 * SWA_BLOCK), 1)
    band = (j_idx > t_idx) & (j_idx <= t_idx + SWA_BLOCK)
    row_head = lax.broadcasted_iota(jnp.int32, (rows, 1), 0) // SWA_BLOCK

    for kvh in range(SWA_KV_HEADS):
        sink = jnp.zeros((rows, 1), F32)
        for g in range(SWA_GROUP):
            sink = jnp.where(row_head == g, sinks_ref[kvh * SWA_GROUP + g], sink)
        for blk in range(nblk):
            r0 = blk * SWA_BLOCK
            parts = []
            for g in range(SWA_GROUP):
                head = kvh * SWA_GROUP + g
                qc = q_cols[head // 2][r0:r0 + SWA_BLOCK, :]
                keep = first_half if head % 2 == 0 else jnp.logical_not(first_half)
                parts.append(jnp.where(keep, qc, 0.0).astype(BF16))
            qs = jnp.concatenate(parts, axis=0)
            keys = kdup[kvh][r0:r0 + 2 * SWA_BLOCK, :]
            vals = vdup[kvh][r0:r0 + 2 * SWA_BLOCK, :]
            s = _dot_nt(qs, keys)
            if blk == 0:
                valid = band & (j_idx >= jnp.where(i > 0, 0, SWA_BLOCK))
            else:
                valid = band
            s = jnp.where(valid, s, -jnp.inf)
            m = jnp.maximum(jnp.max(s, axis=-1, keepdims=True), sink)
            p = jnp.exp(s - m)
            denom = jnp.sum(p, axis=-1, keepdims=True) + jnp.exp(sink - m)
            o = _dot(p.astype(BF16), vals) / denom
            for pair in range(SWA_GROUP // 2):
                a = o[(2 * pair) * SWA_BLOCK:(2 * pair + 1) * SWA_BLOCK, :]
                b = o[(2 * pair + 1) * SWA_BLOCK:(2 * pair + 2) * SWA_BLOCK, :]
                col = (kvh * SWA_GROUP) // 2 + pair
                o_ref[r0:r0 + SWA_BLOCK, col * LANES:(col + 1) * LANES] = (
                    jnp.where(first_half, a, b).astype(o_ref.dtype))

    kbuf[0:SWA_BLOCK, :] = kbuf[tq:tq + SWA_BLOCK, :]
    vbuf[0:SWA_BLOCK, :] = vbuf[tq:tq + SWA_BLOCK, :]


def _swa(qa, ka, va, cos, sin, q_gain, k_gain, sinks, batch, seq):
    nt = seq // SWA_TQ
    tok = lambda b, i, *_: (b * nt + i, 0)
    pos = lambda b, i, *_: (i, 0)
    const = lambda b, i, *_: (0, 0)
    return pl.pallas_call(
        _swa_body,
        grid_spec=pltpu.PrefetchScalarGridSpec(
            num_scalar_prefetch=1,
            grid=(batch, nt),
            in_specs=[
                pl.BlockSpec((SWA_TQ, SWA_Q_W), tok),
                pl.BlockSpec((SWA_TQ, SWA_KV_W), tok),
                pl.BlockSpec((SWA_TQ, SWA_KV_W), tok),
                pl.BlockSpec((SWA_TQ, LANES), pos),
                pl.BlockSpec((SWA_TQ, LANES), pos),
                pl.BlockSpec((1, LANES), const),
                pl.BlockSpec((1, LANES), const),
            ],
            out_specs=pl.BlockSpec((SWA_TQ, SWA_Q_W), tok),
            scratch_shapes=[pltpu.VMEM((SWA_TQ + SWA_BLOCK, LANES), BF16),
                            pltpu.VMEM((SWA_TQ + SWA_BLOCK, LANES), BF16)],
        ),
        out_shape=jax.ShapeDtypeStruct((batch * seq, SWA_Q_W), BF16),
        compiler_params=pltpu.CompilerParams(
            dimension_semantics=("arbitrary", "arbitrary"), vmem_limit_bytes=VMEM_LIMIT),
        name="swa",
    )(sinks, qa, ka, va, cos, sin, q_gain, k_gain)


def _gla_body(q_ref, k_ref, v_ref, r_ref, glr_ref, wg_ref, bias_ref, og_ref, o_ref, s_ref):
    i = pl.program_id(1)
    c = GLA_CHUNK
    nchunk = q_ref.shape[0] // c

    @pl.when(i == 0)
    def _():
        s_ref[...] = jnp.zeros(s_ref.shape, F32)

    t_i = lax.broadcasted_iota(jnp.int32, (c, c), 0)
    s_i = lax.broadcasted_iota(jnp.int32, (c, c), 1)
    causal = s_i <= t_i
    tril = causal.astype(BF16)
    scale = GLA_DK ** -0.5

    def chunk_step(ci, carry):
        r0 = pl.multiple_of(ci * c, c)
        glr = glr_ref[pl.ds(r0, c), :]
        for h in range(GLA_HEADS):
            ks = slice(h * GLA_DK, (h + 1) * GLA_DK)
            vs = slice(h * GLA_DV, (h + 1) * GLA_DV)
            logits = _dot(glr, wg_ref[:, ks]) + bias_ref[:, ks]
            la = (jnp.minimum(logits, 0.0) - jnp.log1p(jnp.exp(-jnp.abs(logits)))) * (1.0 / GLA_TAU)
            b = _split_dot(tril, la)
            e = jnp.exp(b)
            einv = jnp.exp(-b)
            b_last = b[c - 1:c, :]
            b_mid = b[c // 2 - 1:c // 2, :]
            q = q_ref[pl.ds(r0, c), ks].astype(F32) * scale
            k = k_ref[pl.ds(r0, c), ks].astype(F32)
            v = v_ref[pl.ds(r0, c), vs]
            qe = q * e
            q_in = (qe * jnp.exp(-b_mid)).astype(BF16)
            kinv = k * einv
            k_in = (kinv * jnp.exp(b_mid)).astype(BF16)
            k_st = (kinv * jnp.exp(b_last)).astype(BF16)
            state = s_ref[h]
            attn = jnp.where(causal, _dot_nt(q_in, k_in), 0.0)
            o = _dot(qe.astype(BF16), state.astype(BF16)) + _dot(attn.astype(BF16), v)
            decay = jnp.transpose(jnp.broadcast_to(jnp.exp(b_last), (c, GLA_DK)))
            decay = jnp.concatenate([decay] * (GLA_DV // c), axis=1)
            s_ref[h] = decay * state + _dot_tn(k_st, v)
            on = _rms(o, og_ref[:, vs])
            gate = r_ref[pl.ds(r0, c), vs].astype(F32)
            o_ref[pl.ds(r0, c), vs] = (on * gate * jax.nn.sigmoid(gate)).astype(o_ref.dtype)
        return carry

    lax.fori_loop(0, nchunk, chunk_step, 0)


def _gla(qb, kb, vb, rb, glr, wgate, bias, out_gain, layer, batch, seq):
    nt = seq // GLA_TC
    tok = lambda b, i: (b * nt + i, 0)
    return pl.pallas_call(
        _gla_body,
        grid=(batch, nt),
        in_specs=[
            pl.BlockSpec((GLA_TC, GLA_K_W), tok),
            pl.BlockSpec((GLA_TC, GLA_K_W), tok),
            pl.BlockSpec((GLA_TC, GLA_V_W), tok),
            pl.BlockSpec((GLA_TC, GLA_V_W), tok),
            pl.BlockSpec((GLA_TC, GLR_PAD), tok),
            _resident((None, GLR_PAD, GLA_K_W), lambda b, i: (layer, 0, 0)),
            _resident((None, 1, GLA_K_W), lambda b, i: (layer, 0, 0)),
            _resident((None, 1, GLA_V_W), lambda b, i: (layer, 0, 0)),
        ],
        out_specs=pl.BlockSpec((GLA_TC, GLA_V_W), tok),
        out_shape=jax.ShapeDtypeStruct((batch * seq, GLA_V_W), BF16),
        scratch_shapes=[pltpu.VMEM((GLA_HEADS, GLA_DK, GLA_DV), F32)],
        compiler_params=pltpu.CompilerParams(
            dimension_semantics=("arbitrary", "arbitrary"), vmem_limit_bytes=VMEM_LIMIT),
        name="gla",
    )(qb, kb, vb, rb, glr, wgate, bias, out_gain)


def _outproj_body(x_ref, oa_ref, ob_ref, ga_ref, gb_ref, wa_ref, wb_ref, wo_ref, o_ref):
    ya = _dot(oa_ref[...], wa_ref[...])
    yb = _dot(ob_ref[...], wb_ref[...])
    merged = (jax.nn.sigmoid(ga_ref[...].astype(F32)) * ya
              + jax.nn.sigmoid(gb_ref[...].astype(F32)) * yb)
    o_ref[...] = x_ref[...] + _dot(merged.astype(BF16), wo_ref[...])


def _outproj(x, oa, ob, ga, gb, wa, wb, wo, layer):
    n = x.shape[0]
    row = lambda i: (i, 0)
    return pl.pallas_call(
        _outproj_body,
        grid=(n // ROW_TILE,),
        in_specs=[
            pl.BlockSpec((ROW_TILE, D_MODEL), row),
            pl.BlockSpec((ROW_TILE, SWA_Q_W), row),
            pl.BlockSpec((ROW_TILE, GLA_V_W), row),
            pl.BlockSpec((ROW_TILE, D_MODEL), row),
            pl.BlockSpec((ROW_TILE, D_MODEL), row),
            _resident((None, SWA_Q_W, D_MODEL), lambda i: (layer, 0, 0)),
            _resident((None, GLA_V_W, D_MODEL), lambda i: (layer, 0, 0)),
            _resident((None, D_MODEL, D_MODEL), lambda i: (layer, 0, 0)),
        ],
        out_specs=pl.BlockSpec((ROW_TILE, D_MODEL), row),
        out_shape=jax.ShapeDtypeStruct((n, D_MODEL), F32),
        compiler_params=pltpu.CompilerParams(
            dimension_semantics=("arbitrary",), vmem_limit_bytes=VMEM_LIMIT),
        name="outproj",
    )(x, oa, ob, ga, gb, wa, wb, wo)


def _rope_tables(seq):
    half = SWA_HEAD_DIM // 2
    inv_freq = ROPE_THETA ** (-jnp.arange(0, SWA_HEAD_DIM, 2, dtype=F32) / SWA_HEAD_DIM)
    ang = jnp.arange(seq, dtype=F32)[:, None] * inv_freq[None, :]
    cos, sin = jnp.cos(ang), jnp.sin(ang)
    reps = LANES // SWA_HEAD_DIM
    cos_t = jnp.tile(jnp.concatenate([cos, cos], axis=-1), (1, reps))
    sin_t = jnp.tile(jnp.concatenate([-sin, sin], axis=-1), (1, reps))
    assert cos_t.shape == (seq, LANES) and half * 2 == SWA_HEAD_DIM
    return cos_t, sin_t


def kernel(x, ffn1_norm, ffn1_w_gate, ffn1_w_up, ffn1_w_down, mix_norm, w_in, swa_q_norm, swa_k_norm, swa_sinks, gla_w_gate, gla_gate_bias, gla_out_norm, w_proj_a, w_proj_b, w_out, ffn2_norm, ffn2_w_gate, ffn2_w_up, ffn2_w_down):
    batch, seq, d = x.shape
    assert d == D_MODEL and seq % max(SWA_TQ, GLA_TC) == 0 and (batch * seq) % ROW_TILE == 0
    n = batch * seq
    cos_t, sin_t = _rope_tables(seq)

    as3 = lambda g: g.reshape(DEPTH, 1, -1)
    bf = lambda w: w.astype(BF16)
    glr_off = sum(IN_SPLITS[:7])
    w_in_r = jnp.concatenate(
        [w_in[..., :glr_off], w_in[..., glr_off + GLA_GATE_RANK:],
         w_in[..., glr_off:glr_off + GLA_GATE_RANK],
         jnp.zeros((DEPTH, D_MODEL, GLR_PAD - GLA_GATE_RANK), F32)], axis=-1).astype(BF16)
    wgate_p = jnp.concatenate(
        [gla_w_gate, jnp.zeros((DEPTH, GLR_PAD - GLA_GATE_RANK, GLA_K_W), F32)], axis=1).astype(BF16)
    reps = LANES // SWA_HEAD_DIM
    q_gain = jnp.tile(swa_q_norm, (1, reps)).reshape(DEPTH, 1, LANES)
    k_gain = jnp.tile(swa_k_norm, (1, reps)).reshape(DEPTH, 1, LANES)
    f1 = (as3(ffn1_norm), bf(ffn1_w_gate), bf(ffn1_w_up), bf(ffn1_w_down))
    f2 = (as3(ffn2_norm), bf(ffn2_w_gate), bf(ffn2_w_up), bf(ffn2_w_down))
    mix_gain, bias, out_gain = as3(mix_norm), as3(gla_gate_bias), as3(gla_out_norm)
    wa, wb, wo = bf(w_proj_a), bf(w_proj_b), bf(w_out)

    h = x.reshape(n, D_MODEL)
    for l in range(DEPTH):
        h = _ffn(h, *f1, l)
        qa, ka, va, qb, kb, vb, rb, ga, gb, glr = _inproj(h, mix_gain, w_in_r, l)
        oa = _swa(qa, ka, va, cos_t, sin_t, q_gain[l], k_gain[l], swa_sinks[l], batch, seq)
        ob = _gla(qb, kb, vb, rb, glr, wgate_p, bias, out_gain, l, batch, seq)
        h = _outproj(h, oa, ob, ga, gb, wa, wb, wo, l)
        h = _ffn(h, *f2, l)
    return h.reshape(batch, seq, D_MODEL)
```

```python
import jax
import jax.numpy as jnp
from jax import lax
from jax.experimental import pallas as pl
from jax.experimental.pallas import tpu as pltpu

F32 = jnp.float32
BF16 = jnp.bfloat16

D_MODEL = 1024
DEPTH = 4
D_FF = 2816
EPS = 1e-6

SWA_HEADS = 8
SWA_KV_HEADS = 2
SWA_GROUP = SWA_HEADS // SWA_KV_HEADS
SWA_HEAD_DIM = 64
SWA_BLOCK = 128
ROPE_THETA = 10000.0
SWA_Q_W = SWA_HEADS * SWA_HEAD_DIM
SWA_KV_W = SWA_KV_HEADS * SWA_HEAD_DIM

GLA_HEADS = 4
GLA_DK = 128
GLA_DV = 256
GLA_GATE_RANK = 16
GLA_TAU = 16.0
GLA_K_W = GLA_HEADS * GLA_DK
GLA_V_W = GLA_HEADS * GLA_DV

IN_SPLITS = (SWA_Q_W, SWA_KV_W, SWA_KV_W, GLA_K_W, GLA_K_W, GLA_V_W, GLA_V_W, GLA_GATE_RANK, D_MODEL, D_MODEL)

LANES = 128
GLR_PAD = LANES
PROJ_W = (SWA_Q_W, SWA_KV_W, SWA_KV_W, GLR_PAD, GLA_K_W, GLA_K_W, GLA_V_W, GLA_V_W, D_MODEL, D_MODEL)
PROJ_OFF = tuple(sum(PROJ_W[:j]) for j in range(len(PROJ_W)))
PROJ_COLS = sum(PROJ_W)
(P_QA, P_KA, P_VA, P_GLR, P_QB, P_KB, P_VB, P_RB, P_GA, P_GB) = range(len(PROJ_W))
NARROW_W = PROJ_OFF[P_QB]

ROW_TILE = 512
GLA_CHUNK = 128
CHUNKS_PER_TILE = ROW_TILE // GLA_CHUNK
VMEM_LIMIT = 56 * 1024 * 1024


def _rms(x, gain):
    ms = jnp.mean(x * x, axis=-1, keepdims=True)
    return x * lax.rsqrt(ms + EPS) * gain


def _dot(a, b):
    return jnp.dot(a, b, preferred_element_type=F32)


def _dot_nt(a, b):
    return lax.dot_general(a, b, (((1,), (1,)), ((), ())), preferred_element_type=F32)


def _dot_tn(a, b):
    return lax.dot_general(a, b, (((0,), (0,)), ((), ())), preferred_element_type=F32)


def _hi_lo(x):
    hi = x.astype(BF16)
    return hi, (x - hi.astype(F32)).astype(BF16)


def _resident(shape, index_map):
    return pl.BlockSpec(shape, index_map, pipeline_mode=pl.Buffered(1))


def _params(n_axes):
    return pltpu.CompilerParams(dimension_semantics=("arbitrary",) * n_axes,
                                vmem_limit_bytes=VMEM_LIMIT)


def _ffn_body(x_ref, gain_ref, wg_ref, wu_ref, wd_ref, o_ref):
    x = x_ref[...]
    h = _rms(x, gain_ref[...]).astype(BF16)
    g = _dot(h, wg_ref[...])
    u = _dot(h, wu_ref[...])
    a = (g * jax.nn.sigmoid(g) * u).astype(BF16)
    o_ref[...] = x + 0.5 * _dot(a, wd_ref[...])


def _ffn(x, gain, wg, wu, wd, layer):
    n = x.shape[0]
    return pl.pallas_call(
        _ffn_body,
        grid=(n // ROW_TILE,),
        in_specs=[
            pl.BlockSpec((ROW_TILE, D_MODEL), lambda i: (i, 0)),
            _resident((None, 1, D_MODEL), lambda i: (layer, 0, 0)),
            _resident((None, D_MODEL, D_FF), lambda i: (layer, 0, 0)),
            _resident((None, D_MODEL, D_FF), lambda i: (layer, 0, 0)),
            _resident((None, D_FF, D_MODEL), lambda i: (layer, 0, 0)),
        ],
        out_specs=pl.BlockSpec((ROW_TILE, D_MODEL), lambda i: (i, 0)),
        out_shape=jax.ShapeDtypeStruct((n, D_MODEL), F32),
        compiler_params=_params(1),
        name="ffn",
    )(x, gain, wg, wu, wd)


def _head_norm_rope(x, ssq, gain, cos, sin_signed, lane):
    xn = x * lax.rsqrt(ssq * (1.0 / SWA_HEAD_DIM) + EPS) * gain
    half = SWA_HEAD_DIM // 2
    partner = jnp.where((lane % SWA_HEAD_DIM) < half,
                        pltpu.roll(xn, LANES - half, 1), pltpu.roll(xn, half, 1))
    return xn * cos + partner * sin_signed


def _inproj_body(x_ref, gain_ref, w_ref, cos_ref, sin_ref, qg_ref, kg_ref, wgate_ref, bias_ref,
                 ones_ref, qa_ref, ka_ref, va_ref, qe_ref, qin_ref, kin_ref, kst_ref, vb_ref, rb_ref,
                 ga_ref, gb_ref, dec_ref):
    h = _rms(x_ref[...], gain_ref[...]).astype(BF16)
    half = D_MODEL // 2

    def proj(which, lo=0, width=None):
        off = PROJ_OFF[which] + lo
        width = PROJ_W[which] if width is None else width
        return _dot(h, w_ref[:, off:off + width])

    narrow = _dot(h, w_ref[:, :NARROW_W])
    glr = narrow[:, PROJ_OFF[P_GLR]:NARROW_W].astype(BF16)
    logits = _dot(glr, wgate_ref[...]) + bias_ref[...]
    q = proj(P_QB) * GLA_DK ** -0.5
    la = (jnp.minimum(logits, 0.0) - jnp.log1p(jnp.exp(-jnp.abs(logits)))) * (1.0 / GLA_TAU)
    la_hi, la_lo = _hi_lo(la)
    k = proj(P_KB)

    c = GLA_CHUNK
    t_i = lax.broadcasted_iota(jnp.int32, (c, c), 0)
    s_i = lax.broadcasted_iota(jnp.int32, (c, c), 1)
    tril = (s_i <= t_i).astype(BF16)
    cums = []
    for ci in range(CHUNKS_PER_TILE):
        rows = slice(ci * c, (ci + 1) * c)
        cums.append(_dot(tril, la_hi[rows]) + _dot(tril, la_lo[rows]))

    def decay_products(ci):
        rows = slice(ci * c, (ci + 1) * c)
        b = cums[ci]
        e = jnp.exp(b)
        einv = jnp.exp(-b)
        b_last = b[c - 1:c, :]
        b_mid = b[c // 2 - 1:c // 2, :]
        qe = q[rows] * e
        kinv = k[rows] * einv
        qe_ref[rows, :] = qe.astype(BF16)
        qin_ref[rows, :] = (qe * jnp.exp(-b_mid)).astype(BF16)
        kin_ref[rows, :] = (kinv * jnp.exp(b_mid)).astype(BF16)
        kst_ref[rows, :] = (kinv * jnp.exp(b_last)).astype(BF16)
        dec_ref[0, ci:ci + 1, :] = jnp.exp(b_last)

    lane = lax.broadcasted_iota(jnp.int32, (1, LANES), 1)
    cos, sin = cos_ref[...], sin_ref[...]
    qa = narrow[:, :SWA_Q_W]
    ssq_q = _dot((qa * qa).astype(BF16), ones_ref[...])

    def rope_q(col):
        cs = slice(col * LANES, (col + 1) * LANES)
        qc = _head_norm_rope(qa[:, cs], ssq_q[:, cs], qg_ref[...], cos, sin, lane)
        qa_ref[:, cs] = (qc * SWA_HEAD_DIM ** -0.5).astype(BF16)

    vb_ref[:, :half] = proj(P_VB, 0, half).astype(BF16)
    decay_products(0)
    vb_ref[:, half:] = proj(P_VB, half, half).astype(BF16)
    decay_products(1)
    r = proj(P_RB, 0, half)
    decay_products(2)
    rb_ref[:, :half] = (r * jax.nn.sigmoid(r)).astype(BF16)
    r = proj(P_RB, half, half)
    decay_products(3)
    rb_ref[:, half:] = (r * jax.nn.sigmoid(r)).astype(BF16)
    for col, (which, ref, lo) in enumerate(((P_GA, ga_ref, 0), (P_GA, ga_ref, half),
                                             (P_GB, gb_ref, 0), (P_GB, gb_ref, half))):
        ref[:, lo:lo + half] = jax.nn.sigmoid(proj(which, lo, half)).astype(BF16)
        rope_q(col)
    ka = narrow[:, PROJ_OFF[P_KA]:PROJ_OFF[P_VA]]
    ssq_k = _dot((ka * ka).astype(BF16), ones_ref[0:SWA_KV_W, 0:SWA_KV_W])
    ka_ref[...] = _head_norm_rope(ka, ssq_k, kg_ref[...], cos, sin, lane).astype(BF16)
    va_ref[...] = narrow[:, PROJ_OFF[P_VA]:PROJ_OFF[P_GLR]].astype(BF16)


def _inproj(x, gain, w, cos, sin, q_gain, k_gain, wgate, bias, head_ones, layer, seq):
    n = x.shape[0]
    nt = n // ROW_TILE
    tiles_per_seq = seq // ROW_TILE
    row = lambda i: (i, 0)
    pos = lambda i: (i % tiles_per_seq, 0)
    lay = lambda i: (layer, 0, 0)
    widths = (SWA_Q_W, SWA_KV_W, SWA_KV_W, GLA_K_W, GLA_K_W, GLA_K_W, GLA_K_W,
              GLA_V_W, GLA_V_W, D_MODEL, D_MODEL)
    return pl.pallas_call(
        _inproj_body,
        grid=(nt,),
        in_specs=[
            pl.BlockSpec((ROW_TILE, D_MODEL), row),
            _resident((None, 1, D_MODEL), lay),
            _resident((None, D_MODEL, PROJ_COLS), lay),
            pl.BlockSpec((ROW_TILE, LANES), pos),
            pl.BlockSpec((ROW_TILE, LANES), pos),
            _resident((None, 1, LANES), lay),
            _resident((None, 1, LANES), lay),
            _resident((None, GLR_PAD, GLA_K_W), lay),
            _resident((None, 1, GLA_K_W), lay),
            _resident((SWA_Q_W, SWA_Q_W), lambda i: (0, 0)),
        ],
        out_specs=[pl.BlockSpec((ROW_TILE, width), row) for width in widths]
        + [pl.BlockSpec((1, CHUNKS_PER_TILE, GLA_K_W), lambda i: (i, 0, 0))],
        out_shape=[jax.ShapeDtypeStruct((n, width), BF16) for width in widths]
        + [jax.ShapeDtypeStruct((nt, CHUNKS_PER_TILE, GLA_K_W), F32)],
        compiler_params=_params(1),
        name="inproj",
    )(x, gain, w, cos, sin, q_gain, k_gain, wgate, bias, head_ones)


def _swa_body(sinks_ref, q_ref, k_ref, v_ref, o_ref, kbuf, vbuf):
    i = pl.program_id(1)
    tq = q_ref.shape[0]
    nblk = tq // SWA_BLOCK
    lane = lax.broadcasted_iota(jnp.int32, (1, LANES), 1)
    first_half = lane < SWA_HEAD_DIM

    @pl.when(i == 0)
    def _():
        kbuf[0:SWA_BLOCK, :] = jnp.zeros((SWA_BLOCK, LANES), BF16)
        vbuf[0:SWA_BLOCK, :] = jnp.zeros((SWA_BLOCK, LANES), BF16)

    kbuf[SWA_BLOCK:, :] = k_ref[...]
    vbuf[SWA_BLOCK:, :] = v_ref[...]
    kall = kbuf[...]
    vall = vbuf[...]
    kswap = pltpu.roll(kall, SWA_HEAD_DIM, 1)
    vswap = pltpu.roll(vall, SWA_HEAD_DIM, 1)
    kdup = [jnp.where(first_half, kall, kswap), jnp.where(first_half, kswap, kall)]
    vdup = [jnp.where(first_half, vall, vswap), jnp.where(first_half, vswap, vall)]

    rows = SWA_GROUP * SWA_BLOCK
    t_idx = lax.broadcasted_iota(jnp.int32, (rows, SWA_BLOCK), 0) % SWA_BLOCK
    c_idx = lax.broadcasted_iota(jnp.int32, (rows, SWA_BLOCK), 1)
    use_cur = c_idx <= t_idx
    row_head = lax.broadcasted_iota(jnp.int32, (rows, 1), 0) // SWA_BLOCK

    for kvh in range(SWA_KV_HEADS):
        sink = jnp.zeros((rows, 1), F32)
        for g in range(SWA_GROUP):
            sink = jnp.where(row_head == g, sinks_ref[kvh * SWA_GROUP + g], sink)
        for blk in range(nblk):
            r0 = blk * SWA_BLOCK
            parts = []
            for g in range(SWA_GROUP):
                head = kvh * SWA_GROUP + g
                qc = q_ref[r0:r0 + SWA_BLOCK, (head // 2) * LANES:(head // 2 + 1) * LANES]
                keep = first_half if head % 2 == 0 else jnp.logical_not(first_half)
                parts.append(jnp.where(keep, qc, jnp.zeros_like(qc)))
            qs = jnp.concatenate(parts, axis=0)
            keys = kdup[kvh][r0:r0 + 2 * SWA_BLOCK, :]
            vals = vdup[kvh][r0:r0 + 2 * SWA_BLOCK, :]
            s2 = _dot_nt(qs, keys)
            s_prev = s2[:, :SWA_BLOCK]
            if blk == 0:
                s_prev = s_prev + jnp.where(i > 0, 0.0, -jnp.inf)
            s = jnp.where(use_cur, s2[:, SWA_BLOCK:], s_prev)
            m = jnp.maximum(jnp.max(s, axis=-1, keepdims=True), sink)
            p = jnp.exp(s - m)
            denom = jnp.sum(p, axis=-1, keepdims=True) + jnp.exp(sink - m)
            p2 = jnp.concatenate([jnp.where(use_cur, 0.0, p), jnp.where(use_cur, p, 0.0)], axis=1)
            o = _dot(p2.astype(BF16), vals) / denom
            for pair in range(SWA_GROUP // 2):
                a = o[(2 * pair) * SWA_BLOCK:(2 * pair + 1) * SWA_BLOCK, :]
                b = o[(2 * pair + 1) * SWA_BLOCK:(2 * pair + 2) * SWA_BLOCK, :]
                col = (kvh * SWA_GROUP) // 2 + pair
                o_ref[r0:r0 + SWA_BLOCK, col * LANES:(col + 1) * LANES] = (
                    jnp.where(first_half, a, b).astype(o_ref.dtype))

    kbuf[0:SWA_BLOCK, :] = kbuf[tq:tq + SWA_BLOCK, :]
    vbuf[0:SWA_BLOCK, :] = vbuf[tq:tq + SWA_BLOCK, :]


def _swa(qa, ka, va, sinks, batch, seq):
    nt = seq // ROW_TILE
    tok = lambda b, i, *_: (b * nt + i, 0)
    return pl.pallas_call(
        _swa_body,
        grid_spec=pltpu.PrefetchScalarGridSpec(
            num_scalar_prefetch=1,
            grid=(batch, nt),
            in_specs=[
                pl.BlockSpec((ROW_TILE, SWA_Q_W), tok),
                pl.BlockSpec((ROW_TILE, SWA_KV_W), tok),
                pl.BlockSpec((ROW_TILE, SWA_KV_W), tok),
            ],
            out_specs=pl.BlockSpec((ROW_TILE, SWA_Q_W), tok),
            scratch_shapes=[pltpu.VMEM((ROW_TILE + SWA_BLOCK, LANES), BF16),
                            pltpu.VMEM((ROW_TILE + SWA_BLOCK, LANES), BF16)],
        ),
        out_shape=jax.ShapeDtypeStruct((batch * seq, SWA_Q_W), BF16),
        compiler_params=_params(2),
        name="swa",
    )(sinks, qa, ka, va)


def _gla_body(qe_ref, qin_ref, kin_ref, kst_ref, v_ref, dec_ref, o_ref, s_ref):
    i = pl.program_id(1)
    c = GLA_CHUNK

    @pl.when(i == 0)
    def _():
        s_ref[...] = jnp.zeros(s_ref.shape, F32)

    t_i = lax.broadcasted_iota(jnp.int32, (c, c), 0)
    s_i = lax.broadcasted_iota(jnp.int32, (c, c), 1)
    causal = s_i <= t_i

    for h in range(GLA_HEADS):
        ks = slice(h * GLA_DK, (h + 1) * GLA_DK)
        vs = slice(h * GLA_DV, (h + 1) * GLA_DV)
        state = s_ref[h]
        for ci in range(CHUNKS_PER_TILE):
            rows = slice(ci * c, (ci + 1) * c)
            v = v_ref[rows, vs]
            attn = jnp.where(causal, _dot_nt(qin_ref[rows, ks], kin_ref[rows, ks]), 0.0)
            lhs = jnp.concatenate([qe_ref[rows, ks], attn.astype(BF16)], axis=1)
            rhs = jnp.concatenate([state.astype(BF16), v], axis=0)
            o_ref[rows, vs] = _dot(lhs, rhs).astype(o_ref.dtype)
            decay = jnp.transpose(jnp.broadcast_to(dec_ref[0, ci:ci + 1, ks], (c, GLA_DK)))
            decay = jnp.concatenate([decay] * (GLA_DV // c), axis=1)
            state = decay * state + _dot_tn(kst_ref[rows, ks], v)
        s_ref[h] = state


def _gla(qe, qin, kin, kst, vb, dec, batch, seq):
    nt = seq // ROW_TILE
    tok = lambda b, i: (b * nt + i, 0)
    kspec = pl.BlockSpec((ROW_TILE, GLA_K_W), tok)
    vspec = pl.BlockSpec((ROW_TILE, GLA_V_W), tok)
    return pl.pallas_call(
        _gla_body,
        grid=(batch, nt),
        in_specs=[kspec, kspec, kspec, kspec, vspec,
                  pl.BlockSpec((1, CHUNKS_PER_TILE, GLA_K_W), lambda b, i: (b * nt + i, 0, 0))],
        out_specs=vspec,
        out_shape=jax.ShapeDtypeStruct((batch * seq, GLA_V_W), BF16),
        scratch_shapes=[pltpu.VMEM((GLA_HEADS, GLA_DK, GLA_DV), F32)],
        compiler_params=_params(2),
        name="gla",
    )(qe, qin, kin, kst, vb, dec)


def _outproj_body(x_ref, oa_ref, ob_ref, rb_ref, ga_ref, gb_ref, og_ref, wa_ref, wb_ref, wo_ref, o_ref):
    quarter = D_MODEL // GLA_HEADS
    ya = []
    gated = []
    for h in range(GLA_HEADS):
        vs = slice(h * GLA_DV, (h + 1) * GLA_DV)
        ya.append(_dot(oa_ref[...], wa_ref[:, h * quarter:(h + 1) * quarter]))
        on = _rms(ob_ref[:, vs].astype(F32), og_ref[:, vs])
        gated.append((on * rb_ref[:, vs].astype(F32)).astype(BF16))
    gated = jnp.concatenate(gated, axis=1)
    merged = []
    for j in range(GLA_HEADS):
        cs = slice(j * quarter, (j + 1) * quarter)
        yb = _dot(gated, wb_ref[:, cs])
        merged.append((ga_ref[:, cs].astype(F32) * ya[j] + gb_ref[:, cs].astype(F32) * yb).astype(BF16))
    o_ref[...] = x_ref[...] + _dot(jnp.concatenate(merged, axis=1), wo_ref[...])


def _outproj(x, oa, ob, rb, ga, gb, out_gain, wa, wb, wo, layer):
    n = x.shape[0]
    row = lambda i: (i, 0)
    lay = lambda i: (layer, 0, 0)
    return pl.pallas_call(
        _outproj_body,
        grid=(n // ROW_TILE,),
        in_specs=[
            pl.BlockSpec((ROW_TILE, D_MODEL), row),
            pl.BlockSpec((ROW_TILE, SWA_Q_W), row),
            pl.BlockSpec((ROW_TILE, GLA_V_W), row),
            pl.BlockSpec((ROW_TILE, GLA_V_W), row),
            pl.BlockSpec((ROW_TILE, D_MODEL), row),
            pl.BlockSpec((ROW_TILE, D_MODEL), row),
            _resident((None, 1, GLA_V_W), lay),
            _resident((None, SWA_Q_W, D_MODEL), lay),
            _resident((None, GLA_V_W, D_MODEL), lay),
            _resident((None, D_MODEL, D_MODEL), lay),
        ],
        out_specs=pl.BlockSpec((ROW_TILE, D_MODEL), row),
        out_shape=jax.ShapeDtypeStruct((n, D_MODEL), F32),
        compiler_params=_params(1),
        name="outproj",
    )(x, oa, ob, rb, ga, gb, out_gain, wa, wb, wo)


def _rope_tables(seq):
    inv_freq = ROPE_THETA ** (-jnp.arange(0, SWA_HEAD_DIM, 2, dtype=F32) / SWA_HEAD_DIM)
    ang = jnp.arange(seq, dtype=F32)[:, None] * inv_freq[None, :]
    cos, sin = jnp.cos(ang), jnp.sin(ang)
    reps = LANES // SWA_HEAD_DIM
    cos_t = jnp.tile(jnp.concatenate([cos, cos], axis=-1), (1, reps))
    sin_t = jnp.tile(jnp.concatenate([-sin, sin], axis=-1), (1, reps))
    return cos_t, sin_t


def kernel(x, ffn1_norm, ffn1_w_gate, ffn1_w_up, ffn1_w_down, mix_norm, w_in, swa_q_norm, swa_k_norm, swa_sinks, gla_w_gate, gla_gate_bias, gla_out_norm, w_proj_a, w_proj_b, w_out, ffn2_norm, ffn2_w_gate, ffn2_w_up, ffn2_w_down):
    batch, seq, d = x.shape
    assert d == D_MODEL and seq % ROW_TILE == 0
    n = batch * seq
    cos_t, sin_t = _rope_tables(seq)

    as3 = lambda g: g.reshape(DEPTH, 1, -1)
    bf = lambda w: w.astype(BF16)
    qb_off, glr_off = sum(IN_SPLITS[:3]), sum(IN_SPLITS[:7])
    w_in_b = bf(w_in)
    w_in_r = jnp.concatenate(
        [w_in_b[..., :qb_off], w_in_b[..., glr_off:glr_off + GLA_GATE_RANK],
         jnp.zeros((DEPTH, D_MODEL, GLR_PAD - GLA_GATE_RANK), BF16),
         w_in_b[..., qb_off:glr_off], w_in_b[..., glr_off + GLA_GATE_RANK:]], axis=-1)
    wgate_p = jnp.concatenate(
        [gla_w_gate, jnp.zeros((DEPTH, GLR_PAD - GLA_GATE_RANK, GLA_K_W), F32)], axis=1).astype(BF16)
    head_id = jnp.arange(SWA_Q_W, dtype=jnp.int32) // SWA_HEAD_DIM
    head_ones = (head_id[:, None] == head_id[None, :]).astype(BF16)
    reps = LANES // SWA_HEAD_DIM
    q_gain = jnp.tile(swa_q_norm, (1, reps)).reshape(DEPTH, 1, LANES)
    k_gain = jnp.tile(swa_k_norm, (1, reps)).reshape(DEPTH, 1, LANES)
    f1 = (as3(ffn1_norm), bf(ffn1_w_gate), bf(ffn1_w_up), bf(ffn1_w_down))
    f2 = (as3(ffn2_norm), bf(ffn2_w_gate), bf(ffn2_w_up), bf(ffn2_w_down))
    mix_gain, bias, out_gain = as3(mix_norm), as3(gla_gate_bias), as3(gla_out_norm)
    wa, wb, wo = bf(w_proj_a), bf(w_proj_b), bf(w_out)

    h = x.reshape(n, D_MODEL)
    for l in range(DEPTH):
        h = _ffn(h, *f1, l)
        qa, ka, va, qe, qin, kin, kst, vb, rb, ga, gb, dec = _inproj(
            h, mix_gain, w_in_r, cos_t, sin_t, q_gain, k_gain, wgate_p, bias, head_ones, l, seq)
        oa = _swa(qa, ka, va, swa_sinks[l], batch, seq)
        ob = _gla(qe, qin, kin, kst, vb, dec, batch, seq)
        h = _outproj(h, oa, ob, rb, ga, gb, out_gain, wa, wb, wo, l)
        h = _ffn(h, *f2, l)
    return h.reshape(batch, seq, D_MODEL)
```

```python
import jax
import jax.numpy as jnp
from jax import lax
from jax.experimental import pallas as pl
from jax.experimental.pallas import tpu as pltpu

F32 = jnp.float32
BF16 = jnp.bfloat16

D_MODEL = 1024
DEPTH = 4
D_FF = 2816
EPS = 1e-6

SWA_HEADS = 8
SWA_KV_HEADS = 2
SWA_GROUP = SWA_HEADS // SWA_KV_HEADS
SWA_HEAD_DIM = 64
SWA_BLOCK = 128
ROPE_THETA = 10000.0
SWA_Q_W = SWA_HEADS * SWA_HEAD_DIM
SWA_KV_W = SWA_KV_HEADS * SWA_HEAD_DIM

GLA_HEADS = 4
GLA_DK = 128
GLA_DV = 256
GLA_GATE_RANK = 16
GLA_TAU = 16.0
GLA_K_W = GLA_HEADS * GLA_DK
GLA_V_W = GLA_HEADS * GLA_DV

IN_SPLITS = (SWA_Q_W, SWA_KV_W, SWA_KV_W, GLA_K_W, GLA_K_W, GLA_V_W, GLA_V_W, GLA_GATE_RANK, D_MODEL, D_MODEL)

LANES = 128
GLR_PAD = LANES
PROJ_W = (SWA_Q_W, SWA_KV_W, SWA_KV_W, GLR_PAD, GLA_K_W, GLA_K_W, GLA_V_W, GLA_V_W, D_MODEL, D_MODEL)
PROJ_OFF = tuple(sum(PROJ_W[:j]) for j in range(len(PROJ_W)))
PROJ_COLS = sum(PROJ_W)
(P_QA, P_KA, P_VA, P_GLR, P_QB, P_KB, P_VB, P_RB, P_GA, P_GB) = range(len(PROJ_W))
NARROW_W = PROJ_OFF[P_QB]

ROW_TILE = 512
GLA_CHUNK = 128
CHUNKS_PER_TILE = ROW_TILE // GLA_CHUNK
VMEM_LIMIT = 56 * 1024 * 1024


def _rms(x, gain):
    ms = jnp.mean(x * x, axis=-1, keepdims=True)
    return x * lax.rsqrt(ms + EPS) * gain


def _dot(a, b):
    return jnp.dot(a, b, preferred_element_type=F32)


def _dot_nt(a, b):
    return lax.dot_general(a, b, (((1,), (1,)), ((), ())), preferred_element_type=F32)


def _dot_tn(a, b):
    return lax.dot_general(a, b, (((0,), (0,)), ((), ())), preferred_element_type=F32)


def _hi_lo(x):
    hi = x.astype(BF16)
    return hi, (x - hi.astype(F32)).astype(BF16)


def _resident(shape, index_map):
    return pl.BlockSpec(shape, index_map, pipeline_mode=pl.Buffered(1))


def _params(n_axes):
    return pltpu.CompilerParams(dimension_semantics=("arbitrary",) * n_axes,
                                vmem_limit_bytes=VMEM_LIMIT)


BF16_SUBLANES = 16


def _convert_w_in_block(src, dst):
    qb_off, glr_off = sum(IN_SPLITS[:3]), sum(IN_SPLITS[:7])
    wide = glr_off - qb_off
    lane = lax.broadcasted_iota(jnp.int32, (1, LANES), 1)
    dst[:, :qb_off] = src[:, :qb_off].astype(BF16)
    glr = jnp.where(lane < GLA_GATE_RANK, src[:, glr_off:glr_off + LANES], 0.0)
    dst[:, qb_off:NARROW_W] = glr.astype(BF16)
    dst[:, NARROW_W:NARROW_W + wide] = src[:, qb_off:glr_off].astype(BF16)
    dst[:, NARROW_W + wide:] = src[:, glr_off + GLA_GATE_RANK:].astype(BF16)


def _dense_call(body, name, n, in_specs, args, out_specs, out_shapes, convs=()):
    nsteps = n // ROW_TILE
    n_in, n_out, n_conv = len(in_specs), len(out_specs), len(convs)
    in_specs, out_specs, out_shapes, args = list(in_specs), list(out_specs), list(out_shapes), list(args)
    kinds = []
    for w, layer, kind in convs:
        _, rows, cols = w.shape
        nblk = nsteps
        while rows % nblk or (rows // nblk) % BF16_SUBLANES:
            nblk //= 2
        brows, per = rows // nblk, nsteps // nblk
        out_cols = PROJ_COLS if kind == "w_in" else cols
        in_specs.append(pl.BlockSpec((None, brows, cols), lambda i, layer=layer, per=per: (layer, i // per, 0)))
        out_specs.append(pl.BlockSpec((brows, out_cols), lambda i, per=per: (i // per, 0)))
        out_shapes.append(jax.ShapeDtypeStruct((rows, out_cols), BF16))
        args.append(w)
        kinds.append(kind)

    def wrapped(*refs):
        ins, srcs = refs[:n_in], refs[n_in:n_in + n_conv]
        outs, dsts = refs[n_in + n_conv:n_in + n_conv + n_out], refs[n_in + n_conv + n_out:]
        for s, d, kind in zip(srcs, dsts, kinds):
            if kind == "w_in":
                _convert_w_in_block(s, d)
            else:
                d[...] = s[...].astype(BF16)
        body(*ins, *outs)

    res = pl.pallas_call(
        wrapped, grid=(nsteps,), in_specs=in_specs, out_specs=out_specs, out_shape=out_shapes,
        compiler_params=_params(1), name=name)(*args)
    return res[:n_out], res[n_out:]


def _whole(shape):
    return _resident(shape, lambda i: (0,) * len(shape))


def _ffn_body(x_ref, gain_ref, wg_ref, wu_ref, wd_ref, o_ref):
    x = x_ref[...]
    h = _rms(x, gain_ref[...]).astype(BF16)
    g = _dot(h, wg_ref[...])
    u = _dot(h, wu_ref[...])
    a = (g * jax.nn.sigmoid(g) * u).astype(BF16)
    o_ref[...] = x + 0.5 * _dot(a, wd_ref[...])


def _ffn(x, gain, wg, wu, wd, convs=()):
    n = x.shape[0]
    row = pl.BlockSpec((ROW_TILE, D_MODEL), lambda i: (i, 0))
    (out,), converted = _dense_call(
        _ffn_body, "ffn", n,
        [row, _whole((1, D_MODEL)), _whole((D_MODEL, D_FF)), _whole((D_MODEL, D_FF)), _whole((D_FF, D_MODEL))],
        (x, gain, wg, wu, wd), [row], [jax.ShapeDtypeStruct((n, D_MODEL), F32)], convs)
    return out, converted


def _head_norm_rope(x, ssq, gain, cos, sin_signed, lane):
    xn = x * lax.rsqrt(ssq * (1.0 / SWA_HEAD_DIM) + EPS) * gain
    half = SWA_HEAD_DIM // 2
    partner = jnp.where((lane % SWA_HEAD_DIM) < half,
                        pltpu.roll(xn, LANES - half, 1), pltpu.roll(xn, half, 1))
    return xn * cos + partner * sin_signed


def _inproj_body(x_ref, gain_ref, w_ref, cos_ref, sin_ref, qg_ref, kg_ref, wgate_ref, bias_ref,
                 ones_ref, qa_ref, ka_ref, va_ref, qe_ref, qin_ref, kin_ref, kst_ref, vb_ref, rb_ref,
                 ga_ref, gb_ref, dec_ref):
    h = _rms(x_ref[...], gain_ref[...]).astype(BF16)
    half = D_MODEL // 2

    def proj(which, lo=0, width=None):
        off = PROJ_OFF[which] + lo
        width = PROJ_W[which] if width is None else width
        return _dot(h, w_ref[:, off:off + width])

    narrow = _dot(h, w_ref[:, :NARROW_W])
    glr = narrow[:, PROJ_OFF[P_GLR]:NARROW_W].astype(BF16)
    logits = _dot(glr, wgate_ref[...]) + bias_ref[...]
    q = proj(P_QB) * GLA_DK ** -0.5
    la = (jnp.minimum(logits, 0.0) - jnp.log1p(jnp.exp(-jnp.abs(logits)))) * (1.0 / GLA_TAU)
    la_hi, la_lo = _hi_lo(la)
    k = proj(P_KB)

    c = GLA_CHUNK
    t_i = lax.broadcasted_iota(jnp.int32, (c, c), 0)
    s_i = lax.broadcasted_iota(jnp.int32, (c, c), 1)
    tril = (s_i <= t_i).astype(BF16)
    cums = []
    for ci in range(CHUNKS_PER_TILE):
        rows = slice(ci * c, (ci + 1) * c)
        cums.append(_dot(tril, la_hi[rows]) + _dot(tril, la_lo[rows]))

    def decay_products(ci):
        rows = slice(ci * c, (ci + 1) * c)
        b = cums[ci]
        e = jnp.exp(b)
        einv = jnp.exp(-b)
        b_last = b[c - 1:c, :]
        b_mid = b[c // 2 - 1:c // 2, :]
        qe = q[rows] * e
        kinv = k[rows] * einv
        qe_ref[rows, :] = qe.astype(BF16)
        qin_ref[rows, :] = (qe * jnp.exp(-b_mid)).astype(BF16)
        kin_ref[rows, :] = (kinv * jnp.exp(b_mid)).astype(BF16)
        kst_ref[rows, :] = (kinv * jnp.exp(b_last)).astype(BF16)
        dec_ref[0, ci:ci + 1, :] = jnp.exp(b_last)

    lane = lax.broadcasted_iota(jnp.int32, (1, LANES), 1)
    cos, sin = cos_ref[...], sin_ref[...]
    qa = narrow[:, :SWA_Q_W]
    ssq_q = _dot((qa * qa).astype(BF16), ones_ref[...])

    def rope_q(col):
        cs = slice(col * LANES, (col + 1) * LANES)
        qc = _head_norm_rope(qa[:, cs], ssq_q[:, cs], qg_ref[...], cos, sin, lane)
        qa_ref[:, cs] = (qc * SWA_HEAD_DIM ** -0.5).astype(BF16)

    vb_ref[:, :half] = proj(P_VB, 0, half).astype(BF16)
    decay_products(0)
    vb_ref[:, half:] = proj(P_VB, half, half).astype(BF16)
    decay_products(1)
    r = proj(P_RB, 0, half)
    decay_products(2)
    rb_ref[:, :half] = (r * jax.nn.sigmoid(r)).astype(BF16)
    r = proj(P_RB, half, half)
    decay_products(3)
    rb_ref[:, half:] = (r * jax.nn.sigmoid(r)).astype(BF16)
    for col, (which, ref, lo) in enumerate(((P_GA, ga_ref, 0), (P_GA, ga_ref, half),
                                             (P_GB, gb_ref, 0), (P_GB, gb_ref, half))):
        ref[:, lo:lo + half] = jax.nn.sigmoid(proj(which, lo, half)).astype(BF16)
        rope_q(col)
    ka = narrow[:, PROJ_OFF[P_KA]:PROJ_OFF[P_VA]]
    ssq_k = _dot((ka * ka).astype(BF16), ones_ref[0:SWA_KV_W, 0:SWA_KV_W])
    ka_ref[...] = _head_norm_rope(ka, ssq_k, kg_ref[...], cos, sin, lane).astype(BF16)
    va_ref[...] = narrow[:, PROJ_OFF[P_VA]:PROJ_OFF[P_GLR]].astype(BF16)


def _inproj(x, gain, w, cos, sin, q_gain, k_gain, wgate, bias, head_ones, seq, convs=()):
    n = x.shape[0]
    nt = n // ROW_TILE
    tiles_per_seq = seq // ROW_TILE
    row = lambda i: (i, 0)
    pos = lambda i: (i % tiles_per_seq, 0)
    widths = (SWA_Q_W, SWA_KV_W, SWA_KV_W, GLA_K_W, GLA_K_W, GLA_K_W, GLA_K_W,
              GLA_V_W, GLA_V_W, D_MODEL, D_MODEL)
    return _dense_call(
        _inproj_body, "inproj", n,
        [pl.BlockSpec((ROW_TILE, D_MODEL), row), _whole((1, D_MODEL)), _whole((D_MODEL, PROJ_COLS)),
         pl.BlockSpec((ROW_TILE, LANES), pos), pl.BlockSpec((ROW_TILE, LANES), pos),
         _whole((1, LANES)), _whole((1, LANES)), _whole((GLR_PAD, GLA_K_W)), _whole((1, GLA_K_W)),
         _whole((SWA_Q_W, SWA_Q_W))],
        (x, gain, w, cos, sin, q_gain, k_gain, wgate, bias, head_ones),
        [pl.BlockSpec((ROW_TILE, width), row) for width in widths]
        + [pl.BlockSpec((1, CHUNKS_PER_TILE, GLA_K_W), lambda i: (i, 0, 0))],
        [jax.ShapeDtypeStruct((n, width), BF16) for width in widths]
        + [jax.ShapeDtypeStruct((nt, CHUNKS_PER_TILE, GLA_K_W), F32)],
        convs)


def _swa_body(sinks_ref, q_ref, k_ref, v_ref, o_ref, kbuf, vbuf):
    i = pl.program_id(1)
    tq = q_ref.shape[0]
    nblk = tq // SWA_BLOCK
    lane = lax.broadcasted_iota(jnp.int32, (1, LANES), 1)
    first_half = lane < SWA_HEAD_DIM

    @pl.when(i == 0)
    def _():
        kbuf[0:SWA_BLOCK, :] = jnp.zeros((SWA_BLOCK, LANES), BF16)
        vbuf[0:SWA_BLOCK, :] = jnp.zeros((SWA_BLOCK, LANES), BF16)

    kbuf[SWA_BLOCK:, :] = k_ref[...]
    vbuf[SWA_BLOCK:, :] = v_ref[...]
    kall = kbuf[...]
    vall = vbuf[...]
    kswap = pltpu.roll(kall, SWA_HEAD_DIM, 1)
    vswap = pltpu.roll(vall, SWA_HEAD_DIM, 1)
    kdup = [jnp.where(first_half, kall, kswap), jnp.where(first_half, kswap, kall)]
    vdup = [jnp.where(first_half, vall, vswap), jnp.where(first_half, vswap, vall)]

    rows = SWA_GROUP * SWA_BLOCK
    t_idx = lax.broadcasted_iota(jnp.int32, (rows, SWA_BLOCK), 0) % SWA_BLOCK
    c_idx = lax.broadcasted_iota(jnp.int32, (rows, SWA_BLOCK), 1)
    use_cur = c_idx <= t_idx
    row_head = lax.broadcasted_iota(jnp.int32, (rows, 1), 0) // SWA_BLOCK

    for kvh in range(SWA_KV_HEADS):
        sink = jnp.zeros((rows, 1), F32)
        for g in range(SWA_GROUP):
            sink = jnp.where(row_head == g, sinks_ref[kvh * SWA_GROUP + g], sink)
        for blk in range(nblk):
            r0 = blk * SWA_BLOCK
            parts = []
            for g in range(SWA_GROUP):
                head = kvh * SWA_GROUP + g
                qc = q_ref[r0:r0 + SWA_BLOCK, (head // 2) * LANES:(head // 2 + 1) * LANES]
                keep = first_half if head % 2 == 0 else jnp.logical_not(first_half)
                parts.append(jnp.where(keep, qc, jnp.zeros_like(qc)))
            qs = jnp.concatenate(parts, axis=0)
            keys = kdup[kvh][r0:r0 + 2 * SWA_BLOCK, :]
            vals = vdup[kvh][r0:r0 + 2 * SWA_BLOCK, :]
            s2 = _dot_nt(qs, keys)
            s_prev = s2[:, :SWA_BLOCK]
            if blk == 0:
                s_prev = s_prev + jnp.where(i > 0, 0.0, -jnp.inf)
            s = jnp.where(use_cur, s2[:, SWA_BLOCK:], s_prev)
            m = jnp.maximum(jnp.max(s, axis=-1, keepdims=True), sink)
            p = jnp.exp(s - m)
            denom = jnp.sum(p, axis=-1, keepdims=True) + jnp.exp(sink - m)
            p2 = jnp.concatenate([jnp.where(use_cur, 0.0, p), jnp.where(use_cur, p, 0.0)], axis=1)
            o = _dot(p2.astype(BF16), vals) / denom
            for pair in range(SWA_GROUP // 2):
                a = o[(2 * pair) * SWA_BLOCK:(2 * pair + 1) * SWA_BLOCK, :]
                b = o[(2 * pair + 1) * SWA_BLOCK:(2 * pair + 2) * SWA_BLOCK, :]
                col = (kvh * SWA_GROUP) // 2 + pair
                o_ref[r0:r0 + SWA_BLOCK, col * LANES:(col + 1) * LANES] = (
                    jnp.where(first_half, a, b).astype(o_ref.dtype))

    kbuf[0:SWA_BLOCK, :] = kbuf[tq:tq + SWA_BLOCK, :]
    vbuf[0:SWA_BLOCK, :] = vbuf[tq:tq + SWA_BLOCK, :]


def _swa(qa, ka, va, sinks, batch, seq):
    nt = seq // ROW_TILE
    tok = lambda b, i, *_: (b * nt + i, 0)
    return pl.pallas_call(
        _swa_body,
        grid_spec=pltpu.PrefetchScalarGridSpec(
            num_scalar_prefetch=1,
            grid=(batch, nt),
            in_specs=[
                pl.BlockSpec((ROW_TILE, SWA_Q_W), tok),
                pl.BlockSpec((ROW_TILE, SWA_KV_W), tok),
                pl.BlockSpec((ROW_TILE, SWA_KV_W), tok),
            ],
            out_specs=pl.BlockSpec((ROW_TILE, SWA_Q_W), tok),
            scratch_shapes=[pltpu.VMEM((ROW_TILE + SWA_BLOCK, LANES), BF16),
                            pltpu.VMEM((ROW_TILE + SWA_BLOCK, LANES), BF16)],
        ),
        out_shape=jax.ShapeDtypeStruct((batch * seq, SWA_Q_W), BF16),
        compiler_params=_params(2),
        name="swa",
    )(sinks, qa, ka, va)


def _gla_body(qe_ref, qin_ref, kin_ref, kst_ref, v_ref, dec_ref, o_ref, s_ref):
    i = pl.program_id(1)
    c = GLA_CHUNK

    @pl.when(i == 0)
    def _():
        s_ref[...] = jnp.zeros(s_ref.shape, F32)

    t_i = lax.broadcasted_iota(jnp.int32, (c, c), 0)
    s_i = lax.broadcasted_iota(jnp.int32, (c, c), 1)
    causal = s_i <= t_i

    for h in range(GLA_HEADS):
        ks = slice(h * GLA_DK, (h + 1) * GLA_DK)
        vs = slice(h * GLA_DV, (h + 1) * GLA_DV)
        state = s_ref[h]
        for ci in range(CHUNKS_PER_TILE):
            rows = slice(ci * c, (ci + 1) * c)
            v = v_ref[rows, vs]
            attn = jnp.where(causal, _dot_nt(qin_ref[rows, ks], kin_ref[rows, ks]), 0.0)
            lhs = jnp.concatenate([qe_ref[rows, ks], attn.astype(BF16)], axis=1)
            rhs = jnp.concatenate([state.astype(BF16), v], axis=0)
            o_ref[rows, vs] = _dot(lhs, rhs).astype(o_ref.dtype)
            decay = jnp.transpose(jnp.broadcast_to(dec_ref[0, ci:ci + 1, ks], (c, GLA_DK)))
            decay = jnp.concatenate([decay] * (GLA_DV // c), axis=1)
            state = decay * state + _dot_tn(kst_ref[rows, ks], v)
        s_ref[h] = state


def _gla(qe, qin, kin, kst, vb, dec, batch, seq):
    nt = seq // ROW_TILE
    tok = lambda b, i: (b * nt + i, 0)
    kspec = pl.BlockSpec((ROW_TILE, GLA_K_W), tok)
    vspec = pl.BlockSpec((ROW_TILE, GLA_V_W), tok)
    return pl.pallas_call(
        _gla_body,
        grid=(batch, nt),
        in_specs=[kspec, kspec, kspec, kspec, vspec,
                  pl.BlockSpec((1, CHUNKS_PER_TILE, GLA_K_W), lambda b, i: (b * nt + i, 0, 0))],
        out_specs=vspec,
        out_shape=jax.ShapeDtypeStruct((batch * seq, GLA_V_W), BF16),
        scratch_shapes=[pltpu.VMEM((GLA_HEADS, GLA_DK, GLA_DV), F32)],
        compiler_params=_params(2),
        name="gla",
    )(qe, qin, kin, kst, vb, dec)


def _outproj_body(x_ref, oa_ref, ob_ref, rb_ref, ga_ref, gb_ref, og_ref, wa_ref, wb_ref, wo_ref, o_ref):
    quarter = D_MODEL // GLA_HEADS
    ya = []
    gated = []
    for h in range(GLA_HEADS):
        vs = slice(h * GLA_DV, (h + 1) * GLA_DV)
        ya.append(_dot(oa_ref[...], wa_ref[:, h * quarter:(h + 1) * quarter]))
        on = _rms(ob_ref[:, vs].astype(F32), og_ref[:, vs])
        gated.append((on * rb_ref[:, vs].astype(F32)).astype(BF16))
    gated = jnp.concatenate(gated, axis=1)
    merged = []
    for j in range(GLA_HEADS):
        cs = slice(j * quarter, (j + 1) * quarter)
        yb = _dot(gated, wb_ref[:, cs])
        merged.append((ga_ref[:, cs].astype(F32) * ya[j] + gb_ref[:, cs].astype(F32) * yb).astype(BF16))
    o_ref[...] = x_ref[...] + _dot(jnp.concatenate(merged, axis=1), wo_ref[...])


def _outproj(x, oa, ob, rb, ga, gb, out_gain, wa, wb, wo, convs=()):
    n = x.shape[0]
    spec = lambda width: pl.BlockSpec((ROW_TILE, width), lambda i: (i, 0))
    (out,), converted = _dense_call(
        _outproj_body, "outproj", n,
        [spec(D_MODEL), spec(SWA_Q_W), spec(GLA_V_W), spec(GLA_V_W), spec(D_MODEL), spec(D_MODEL),
         _whole((1, GLA_V_W)), _whole((SWA_Q_W, D_MODEL)), _whole((GLA_V_W, D_MODEL)),
         _whole((D_MODEL, D_MODEL))],
        (x, oa, ob, rb, ga, gb, out_gain, wa, wb, wo),
        [spec(D_MODEL)], [jax.ShapeDtypeStruct((n, D_MODEL), F32)], convs)
    return out, converted


def _rope_tables(seq):
    inv_freq = ROPE_THETA ** (-jnp.arange(0, SWA_HEAD_DIM, 2, dtype=F32) / SWA_HEAD_DIM)
    ang = jnp.arange(seq, dtype=F32)[:, None] * inv_freq[None, :]
    cos, sin = jnp.cos(ang), jnp.sin(ang)
    reps = LANES // SWA_HEAD_DIM
    cos_t = jnp.tile(jnp.concatenate([cos, cos], axis=-1), (1, reps))
    sin_t = jnp.tile(jnp.concatenate([-sin, sin], axis=-1), (1, reps))
    return cos_t, sin_t


def kernel(x, ffn1_norm, ffn1_w_gate, ffn1_w_up, ffn1_w_down, mix_norm, w_in, swa_q_norm, swa_k_norm, swa_sinks, gla_w_gate, gla_gate_bias, gla_out_norm, w_proj_a, w_proj_b, w_out, ffn2_norm, ffn2_w_gate, ffn2_w_up, ffn2_w_down):
    batch, seq, d = x.shape
    assert d == D_MODEL and seq % ROW_TILE == 0
    n = batch * seq
    cos_t, sin_t = _rope_tables(seq)

    bf = lambda w: w.astype(BF16)
    wgate_p = jnp.concatenate(
        [gla_w_gate, jnp.zeros((DEPTH, GLR_PAD - GLA_GATE_RANK, GLA_K_W), F32)], axis=1).astype(BF16)
    head_id = jnp.arange(SWA_Q_W, dtype=jnp.int32) // SWA_HEAD_DIM
    head_ones = (head_id[:, None] == head_id[None, :]).astype(BF16)
    reps = LANES // SWA_HEAD_DIM
    q_gain = jnp.tile(swa_q_norm, (1, reps))
    k_gain = jnp.tile(swa_k_norm, (1, reps))
    row = lambda g, l: g[l:l + 1]

    f1 = (bf(ffn1_w_gate[0]), bf(ffn1_w_up[0]), bf(ffn1_w_down[0]))
    h = x.reshape(n, D_MODEL)
    for l in range(DEPTH):
        h, (w_in_r,) = _ffn(h, row(ffn1_norm, l), *f1, convs=[(w_in, l, "w_in")])
        (qa, ka, va, qe, qin, kin, kst, vb, rb, ga, gb, dec), (wa, wb, wo) = _inproj(
            h, row(mix_norm, l), w_in_r, cos_t, sin_t, row(q_gain, l), row(k_gain, l), wgate_p[l],
            row(gla_gate_bias, l), head_ones, seq,
            convs=[(w_proj_a, l, "plain"), (w_proj_b, l, "plain"), (w_out, l, "plain")])
        oa = _swa(qa, ka, va, swa_sinks[l], batch, seq)
        ob = _gla(qe, qin, kin, kst, vb, dec, batch, seq)
        h, f2 = _outproj(
            h, oa, ob, rb, ga, gb, row(gla_out_norm, l), wa, wb, wo,
            convs=[(ffn2_w_gate, l, "plain"), (ffn2_w_up, l, "plain"), (ffn2_w_down, l, "plain")])
        nxt = [] if l + 1 == DEPTH else [(ffn1_w_gate, l + 1, "plain"), (ffn1_w_up, l + 1, "plain"),
                                         (ffn1_w_down, l + 1, "plain")]
        h, f1 = _ffn(h, row(ffn2_norm, l), *f2, convs=nxt)
    return h.reshape(batch, seq, D_MODEL)
```

```python
import jax
import jax.numpy as jnp
from jax import lax
from jax.experimental import pallas as pl
from jax.experimental.pallas import tpu as pltpu

F32 = jnp.float32
BF16 = jnp.bfloat16

D_MODEL = 1024
DEPTH = 4
D_FF = 2816
EPS = 1e-6

SWA_HEADS = 8
SWA_KV_HEADS = 2
SWA_GROUP = SWA_HEADS // SWA_KV_HEADS
SWA_HEAD_DIM = 64
SWA_BLOCK = 128
ROPE_THETA = 10000.0
SWA_Q_W = SWA_HEADS * SWA_HEAD_DIM
SWA_KV_W = SWA_KV_HEADS * SWA_HEAD_DIM

GLA_HEADS = 4
GLA_DK = 128
GLA_DV = 256
GLA_GATE_RANK = 16
GLA_TAU = 16.0
GLA_K_W = GLA_HEADS * GLA_DK
GLA_V_W = GLA_HEADS * GLA_DV

IN_SPLITS = (SWA_Q_W, SWA_KV_W, SWA_KV_W, GLA_K_W, GLA_K_W, GLA_V_W, GLA_V_W, GLA_GATE_RANK, D_MODEL, D_MODEL)

LANES = 128
GLR_PAD = LANES
IN_COLS = sum(IN_SPLITS)
IN_OFF = tuple(sum(IN_SPLITS[:j]) for j in range(len(IN_SPLITS)))
(P_QA, P_KA, P_VA, P_QB, P_KB, P_VB, P_RB, P_GLR, P_GA, P_GB) = range(len(IN_SPLITS))

ROW_TILE = 512
GLA_CHUNK = 128
CHUNKS_PER_TILE = ROW_TILE // GLA_CHUNK
VMEM_LIMIT = 56 * 1024 * 1024


def _rms(x, gain):
    ms = jnp.mean(x * x, axis=-1, keepdims=True)
    return x * lax.rsqrt(ms + EPS) * gain


def _dot(a, b):
    return jnp.dot(a, b, preferred_element_type=F32)


def _dot_nt(a, b):
    return lax.dot_general(a, b, (((1,), (1,)), ((), ())), preferred_element_type=F32)


def _dot_tn(a, b):
    return lax.dot_general(a, b, (((0,), (0,)), ((), ())), preferred_element_type=F32)


def _hi_lo(x):
    hi = x.astype(BF16)
    return hi, (x - hi.astype(F32)).astype(BF16)


def _resident(shape, index_map):
    return pl.BlockSpec(shape, index_map, pipeline_mode=pl.Buffered(1))


def _params(n_axes):
    return pltpu.CompilerParams(dimension_semantics=("arbitrary",) * n_axes,
                                vmem_limit_bytes=VMEM_LIMIT)


BF16_SUBLANES = 16


def _dense_call(body, name, n, in_specs, args, out_specs, out_shapes, convs=None):
    nsteps = n // ROW_TILE
    n_in, n_out, n_conv = len(in_specs), len(out_specs), len(convs or ())
    in_specs, out_specs, out_shapes, args = list(in_specs), list(out_specs), list(out_shapes), list(args)
    for w, layer in convs or ():
        _, rows, cols = w.shape
        nblk = max(b for b in range(1, nsteps + 1) if rows % (b * BF16_SUBLANES) == 0)
        last = nblk - 1
        in_specs.append(pl.BlockSpec((None, rows // nblk, cols),
                                     lambda i, layer=layer, last=last: (layer, jnp.minimum(i, last), 0)))
        out_specs.append(pl.BlockSpec((rows // nblk, cols), lambda i, last=last: (jnp.minimum(i, last), 0)))
        out_shapes.append(jax.ShapeDtypeStruct((rows, cols), BF16))
        args.append(w)

    def wrapped(*refs):
        ins, srcs = refs[:n_in], refs[n_in:n_in + n_conv]
        outs, dsts = refs[n_in + n_conv:n_in + n_conv + n_out], refs[n_in + n_conv + n_out:]

        def cast_weights():
            for s, d in zip(srcs, dsts):
                d[...] = s[...].astype(BF16)

        if convs is None:
            body(*ins, *outs)
        else:
            body(cast_weights, *ins, *outs)

    res = pl.pallas_call(
        wrapped, grid=(nsteps,), in_specs=in_specs, out_specs=out_specs, out_shape=out_shapes,
        compiler_params=_params(1), name=name)(*args)
    return res[:n_out], res[n_out:]


def _whole(shape):
    return _resident(shape, lambda i: (0,) * len(shape))


def _ffn_body(cast_weights, x_ref, gain_ref, wg_ref, wu_ref, wd_ref, o_ref):
    x = x_ref[...]
    h = _rms(x, gain_ref[...]).astype(BF16)
    g = _dot(h, wg_ref[...])
    cast_weights()
    u = _dot(h, wu_ref[...])
    a = (g * jax.nn.sigmoid(g) * u).astype(BF16)
    o_ref[...] = x + 0.5 * _dot(a, wd_ref[...])


def _ffn(x, gain, wg, wu, wd, convs=()):
    n = x.shape[0]
    row = pl.BlockSpec((ROW_TILE, D_MODEL), lambda i: (i, 0))
    (out,), converted = _dense_call(
        _ffn_body, "ffn", n,
        [row, _whole((1, D_MODEL)), _whole((D_MODEL, D_FF)), _whole((D_MODEL, D_FF)), _whole((D_FF, D_MODEL))],
        (x, gain, wg, wu, wd), [row], [jax.ShapeDtypeStruct((n, D_MODEL), F32)], convs)
    return out, converted


def _head_norm_rope(x, ssq, gain, cos, sin_signed, lane):
    xn = x * lax.rsqrt(ssq * (1.0 / SWA_HEAD_DIM) + EPS) * gain
    half = SWA_HEAD_DIM // 2
    partner = jnp.where((lane % SWA_HEAD_DIM) < half,
                        pltpu.roll(xn, LANES - half, 1), pltpu.roll(xn, half, 1))
    return xn * cos + partner * sin_signed


def _inproj_body(x_ref, gain_ref, w_ref, cos_ref, sin_ref, qg_ref, kg_ref, wgate_ref, bias_ref,
                 ones_ref, qa_ref, ka_ref, va_ref, qe_ref, qin_ref, kin_ref, kst_ref, vb_ref, rb_ref,
                 ga_ref, gb_ref, dec_ref):
    h = _rms(x_ref[...], gain_ref[...]).astype(BF16)
    half = D_MODEL // 2
    lane = lax.broadcasted_iota(jnp.int32, (1, LANES), 1)

    def proj(which, lo=0, width=None):
        off = IN_OFF[which] + lo
        width = IN_SPLITS[which] if width is None else width
        return _dot_nt(h, w_ref[off:off + width, :])

    narrow = proj(P_QA, 0, IN_OFF[P_QB])
    tail = proj(P_RB, half, half + LANES)
    glr = jnp.where(lane < GLA_GATE_RANK, tail[:, half:], 0.0).astype(BF16)
    logits = _dot(glr, wgate_ref[...]) + bias_ref[...]
    r = tail[:, :half]
    rb_ref[:, half:] = (r * jax.nn.sigmoid(r)).astype(BF16)
    q = proj(P_QB) * GLA_DK ** -0.5
    la = (jnp.minimum(logits, 0.0) - jnp.log1p(jnp.exp(-jnp.abs(logits)))) * (1.0 / GLA_TAU)
    la_hi, la_lo = _hi_lo(la)
    k = proj(P_KB)

    c = GLA_CHUNK
    t_i = lax.broadcasted_iota(jnp.int32, (c, c), 0)
    s_i = lax.broadcasted_iota(jnp.int32, (c, c), 1)
    tril = (s_i <= t_i).astype(BF16)
    cums = []
    for ci in range(CHUNKS_PER_TILE):
        rows = slice(ci * c, (ci + 1) * c)
        cums.append(_dot(tril, la_hi[rows]) + _dot(tril, la_lo[rows]))

    def decay_products(ci):
        rows = slice(ci * c, (ci + 1) * c)
        b = cums[ci]
        e = jnp.exp(b)
        einv = jnp.exp(-b)
        b_last = b[c - 1:c, :]
        b_mid = b[c // 2 - 1:c // 2, :]
        qe = q[rows] * e
        kinv = k[rows] * einv
        qe_ref[rows, :] = qe.astype(BF16)
        qin_ref[rows, :] = (qe * jnp.exp(-b_mid)).astype(BF16)
        kin_ref[rows, :] = (kinv * jnp.exp(b_mid)).astype(BF16)
        kst_ref[rows, :] = (kinv * jnp.exp(b_last)).astype(BF16)
        dec_ref[0, ci:ci + 1, :] = jnp.exp(b_last)

    cos, sin = cos_ref[...], sin_ref[...]
    qa = narrow[:, :SWA_Q_W]
    ssq_q = _dot((qa * qa).astype(BF16), ones_ref[...])

    def rope_q(col):
        cs = slice(col * LANES, (col + 1) * LANES)
        qc = _head_norm_rope(qa[:, cs], ssq_q[:, cs], qg_ref[...], cos, sin, lane)
        qa_ref[:, cs] = (qc * SWA_HEAD_DIM ** -0.5).astype(BF16)

    vb_ref[:, :half] = proj(P_VB, 0, half).astype(BF16)
    decay_products(0)
    vb_ref[:, half:] = proj(P_VB, half, half).astype(BF16)
    decay_products(1)
    r = proj(P_RB, 0, half)
    decay_products(2)
    rb_ref[:, :half] = (r * jax.nn.sigmoid(r)).astype(BF16)
    decay_products(3)
    for col, (which, ref, lo) in enumerate(((P_GA, ga_ref, 0), (P_GA, ga_ref, half),
                                             (P_GB, gb_ref, 0), (P_GB, gb_ref, half))):
        ref[:, lo:lo + half] = jax.nn.sigmoid(proj(which, lo, half)).astype(BF16)
        rope_q(col)
    ka = narrow[:, IN_OFF[P_KA]:IN_OFF[P_VA]]
    ssq_k = _dot((ka * ka).astype(BF16), ones_ref[0:SWA_KV_W, 0:SWA_KV_W])
    ka_ref[...] = _head_norm_rope(ka, ssq_k, kg_ref[...], cos, sin, lane).astype(BF16)
    va_ref[...] = narrow[:, IN_OFF[P_VA]:IN_OFF[P_QB]].astype(BF16)


def _inproj(x, gain, w, cos, sin, q_gain, k_gain, wgate, bias, head_ones, seq):
    n = x.shape[0]
    nt = n // ROW_TILE
    tiles_per_seq = seq // ROW_TILE
    row = lambda i: (i, 0)
    pos = lambda i: (i % tiles_per_seq, 0)
    widths = (SWA_Q_W, SWA_KV_W, SWA_KV_W, GLA_K_W, GLA_K_W, GLA_K_W, GLA_K_W,
              GLA_V_W, GLA_V_W, D_MODEL, D_MODEL)
    outs, _ = _dense_call(
        _inproj_body, "inproj", n,
        [pl.BlockSpec((ROW_TILE, D_MODEL), row), _whole((1, D_MODEL)), _whole((IN_COLS, D_MODEL)),
         pl.BlockSpec((ROW_TILE, LANES), pos), pl.BlockSpec((ROW_TILE, LANES), pos),
         _whole((1, LANES)), _whole((1, LANES)), _whole((GLR_PAD, GLA_K_W)), _whole((1, GLA_K_W)),
         _whole((SWA_Q_W, SWA_Q_W))],
        (x, gain, w, cos, sin, q_gain, k_gain, wgate, bias, head_ones),
        [pl.BlockSpec((ROW_TILE, width), row) for width in widths]
        + [pl.BlockSpec((1, CHUNKS_PER_TILE, GLA_K_W), lambda i: (i, 0, 0))],
        [jax.ShapeDtypeStruct((n, width), BF16) for width in widths]
        + [jax.ShapeDtypeStruct((nt, CHUNKS_PER_TILE, GLA_K_W), F32)])
    return outs


def _swa_body(sinks_ref, q_ref, k_ref, v_ref, o_ref, kbuf, vbuf):
    i = pl.program_id(1)
    tq = q_ref.shape[0]
    nblk = tq // SWA_BLOCK
    lane = lax.broadcasted_iota(jnp.int32, (1, LANES), 1)
    first_half = lane < SWA_HEAD_DIM

    @pl.when(i == 0)
    def _():
        kbuf[0:SWA_BLOCK, :] = jnp.zeros((SWA_BLOCK, LANES), BF16)
        vbuf[0:SWA_BLOCK, :] = jnp.zeros((SWA_BLOCK, LANES), BF16)

    kbuf[SWA_BLOCK:, :] = k_ref[...]
    vbuf[SWA_BLOCK:, :] = v_ref[...]
    kall = kbuf[...]
    vall = vbuf[...]
    kswap = pltpu.roll(kall, SWA_HEAD_DIM, 1)
    vswap = pltpu.roll(vall, SWA_HEAD_DIM, 1)
    kdup = [jnp.where(first_half, kall, kswap), jnp.where(first_half, kswap, kall)]
    vdup = [jnp.where(first_half, vall, vswap), jnp.where(first_half, vswap, vall)]

    rows = SWA_GROUP * SWA_BLOCK
    t_idx = lax.broadcasted_iota(jnp.int32, (rows, SWA_BLOCK), 0) % SWA_BLOCK
    c_idx = lax.broadcasted_iota(jnp.int32, (rows, SWA_BLOCK), 1)
    use_cur = c_idx <= t_idx
    row_head = lax.broadcasted_iota(jnp.int32, (rows, 1), 0) // SWA_BLOCK

    for kvh in range(SWA_KV_HEADS):
        sink = jnp.zeros((rows, 1), F32)
        for g in range(SWA_GROUP):
            sink = jnp.where(row_head == g, sinks_ref[kvh * SWA_GROUP + g], sink)
        for blk in range(nblk):
            r0 = blk * SWA_BLOCK
            parts = []
            for g in range(SWA_GROUP):
                head = kvh * SWA_GROUP + g
                qc = q_ref[r0:r0 + SWA_BLOCK, (head // 2) * LANES:(head // 2 + 1) * LANES]
                keep = first_half if head % 2 == 0 else jnp.logical_not(first_half)
                parts.append(jnp.where(keep, qc, jnp.zeros_like(qc)))
            qs = jnp.concatenate(parts, axis=0)
            keys = kdup[kvh][r0:r0 + 2 * SWA_BLOCK, :]
            vals = vdup[kvh][r0:r0 + 2 * SWA_BLOCK, :]
            s2 = _dot_nt(qs, keys)
            s_prev = s2[:, :SWA_BLOCK]
            if blk == 0:
                s_prev = s_prev + jnp.where(i > 0, 0.0, -jnp.inf)
            s = jnp.where(use_cur, s2[:, SWA_BLOCK:], s_prev)
            m = jnp.maximum(jnp.max(s, axis=-1, keepdims=True), sink)
            p = jnp.exp(s - m)
            denom = jnp.sum(p, axis=-1, keepdims=True) + jnp.exp(sink - m)
            p2 = jnp.concatenate([jnp.where(use_cur, 0.0, p), jnp.where(use_cur, p, 0.0)], axis=1)
            o = _dot(p2.astype(BF16), vals) / denom
            for pair in range(SWA_GROUP // 2):
                a = o[(2 * pair) * SWA_BLOCK:(2 * pair + 1) * SWA_BLOCK, :]
                b = o[(2 * pair + 1) * SWA_BLOCK:(2 * pair + 2) * SWA_BLOCK, :]
                col = (kvh * SWA_GROUP) // 2 + pair
                o_ref[r0:r0 + SWA_BLOCK, col * LANES:(col + 1) * LANES] = (
                    jnp.where(first_half, a, b).astype(o_ref.dtype))

    kbuf[0:SWA_BLOCK, :] = kbuf[tq:tq + SWA_BLOCK, :]
    vbuf[0:SWA_BLOCK, :] = vbuf[tq:tq + SWA_BLOCK, :]


def _swa(qa, ka, va, sinks, batch, seq):
    nt = seq // ROW_TILE
    tok = lambda b, i, *_: (b * nt + i, 0)
    return pl.pallas_call(
        _swa_body,
        grid_spec=pltpu.PrefetchScalarGridSpec(
            num_scalar_prefetch=1,
            grid=(batch, nt),
            in_specs=[
                pl.BlockSpec((ROW_TILE, SWA_Q_W), tok),
                pl.BlockSpec((ROW_TILE, SWA_KV_W), tok),
                pl.BlockSpec((ROW_TILE, SWA_KV_W), tok),
            ],
            out_specs=pl.BlockSpec((ROW_TILE, SWA_Q_W), tok),
            scratch_shapes=[pltpu.VMEM((ROW_TILE + SWA_BLOCK, LANES), BF16),
                            pltpu.VMEM((ROW_TILE + SWA_BLOCK, LANES), BF16)],
        ),
        out_shape=jax.ShapeDtypeStruct((batch * seq, SWA_Q_W), BF16),
        compiler_params=_params(2),
        name="swa",
    )(sinks, qa, ka, va)


def _gla_body(qe_ref, qin_ref, kin_ref, kst_ref, v_ref, dec_ref, o_ref, s_ref):
    i = pl.program_id(1)
    c = GLA_CHUNK

    @pl.when(i == 0)
    def _():
        s_ref[...] = jnp.zeros(s_ref.shape, F32)

    t_i = lax.broadcasted_iota(jnp.int32, (c, c), 0)
    s_i = lax.broadcasted_iota(jnp.int32, (c, c), 1)
    causal = s_i <= t_i

    for h in range(GLA_HEADS):
        ks = slice(h * GLA_DK, (h + 1) * GLA_DK)
        vs = slice(h * GLA_DV, (h + 1) * GLA_DV)
        state = s_ref[h]
        for ci in range(CHUNKS_PER_TILE):
            rows = slice(ci * c, (ci + 1) * c)
            v = v_ref[rows, vs]
            attn = jnp.where(causal, _dot_nt(qin_ref[rows, ks], kin_ref[rows, ks]), 0.0)
            lhs = jnp.concatenate([qe_ref[rows, ks], attn.astype(BF16)], axis=1)
            rhs = jnp.concatenate([state.astype(BF16), v], axis=0)
            o_ref[rows, vs] = _dot(lhs, rhs).astype(o_ref.dtype)
            decay = jnp.transpose(jnp.broadcast_to(dec_ref[0, ci:ci + 1, ks], (c, GLA_DK)))
            decay = jnp.concatenate([decay] * (GLA_DV // c), axis=1)
            state = decay * state + _dot_tn(kst_ref[rows, ks], v)
        s_ref[h] = state


def _gla(qe, qin, kin, kst, vb, dec, batch, seq):
    nt = seq // ROW_TILE
    tok = lambda b, i: (b * nt + i, 0)
    kspec = pl.BlockSpec((ROW_TILE, GLA_K_W), tok)
    vspec = pl.BlockSpec((ROW_TILE, GLA_V_W), tok)
    return pl.pallas_call(
        _gla_body,
        grid=(batch, nt),
        in_specs=[kspec, kspec, kspec, kspec, vspec,
                  pl.BlockSpec((1, CHUNKS_PER_TILE, GLA_K_W), lambda b, i: (b * nt + i, 0, 0))],
        out_specs=vspec,
        out_shape=jax.ShapeDtypeStruct((batch * seq, GLA_V_W), BF16),
        scratch_shapes=[pltpu.VMEM((GLA_HEADS, GLA_DK, GLA_DV), F32)],
        compiler_params=_params(2),
        name="gla",
    )(qe, qin, kin, kst, vb, dec)


def _outproj_body(x_ref, oa_ref, ob_ref, rb_ref, ga_ref, gb_ref, og_ref, wa_ref, wb_ref, wo_ref, o_ref):
    quarter = D_MODEL // GLA_HEADS
    ya = []
    gated = []
    for h in range(GLA_HEADS):
        vs = slice(h * GLA_DV, (h + 1) * GLA_DV)
        ya.append(_dot(oa_ref[...], wa_ref[:, h * quarter:(h + 1) * quarter]))
        on = _rms(ob_ref[:, vs].astype(F32), og_ref[:, vs])
        gated.append((on * rb_ref[:, vs].astype(F32)).astype(BF16))
    gated = jnp.concatenate(gated, axis=1)
    merged = []
    for j in range(GLA_HEADS):
        cs = slice(j * quarter, (j + 1) * quarter)
        yb = _dot(gated, wb_ref[:, cs])
        merged.append((ga_ref[:, cs].astype(F32) * ya[j] + gb_ref[:, cs].astype(F32) * yb).astype(BF16))
    o_ref[...] = x_ref[...] + _dot(jnp.concatenate(merged, axis=1), wo_ref[...])


def _outproj(x, oa, ob, rb, ga, gb, out_gain, wa, wb, wo):
    n = x.shape[0]
    spec = lambda width: pl.BlockSpec((ROW_TILE, width), lambda i: (i, 0))
    (out,), _ = _dense_call(
        _outproj_body, "outproj", n,
        [spec(D_MODEL), spec(SWA_Q_W), spec(GLA_V_W), spec(GLA_V_W), spec(D_MODEL), spec(D_MODEL),
         _whole((1, GLA_V_W)), _whole((SWA_Q_W, D_MODEL)), _whole((GLA_V_W, D_MODEL)),
         _whole((D_MODEL, D_MODEL))],
        (x, oa, ob, rb, ga, gb, out_gain, wa, wb, wo),
        [spec(D_MODEL)], [jax.ShapeDtypeStruct((n, D_MODEL), F32)])
    return out


def _rope_tables(seq):
    inv_freq = ROPE_THETA ** (-jnp.arange(0, SWA_HEAD_DIM, 2, dtype=F32) / SWA_HEAD_DIM)
    ang = jnp.arange(seq, dtype=F32)[:, None] * inv_freq[None, :]
    cos, sin = jnp.cos(ang), jnp.sin(ang)
    reps = LANES // SWA_HEAD_DIM
    cos_t = jnp.tile(jnp.concatenate([cos, cos], axis=-1), (1, reps))
    sin_t = jnp.tile(jnp.concatenate([-sin, sin], axis=-1), (1, reps))
    return cos_t, sin_t


def kernel(x, ffn1_norm, ffn1_w_gate, ffn1_w_up, ffn1_w_down, mix_norm, w_in, swa_q_norm, swa_k_norm, swa_sinks, gla_w_gate, gla_gate_bias, gla_out_norm, w_proj_a, w_proj_b, w_out, ffn2_norm, ffn2_w_gate, ffn2_w_up, ffn2_w_down):
    batch, seq, d = x.shape
    assert d == D_MODEL and seq % ROW_TILE == 0
    n = batch * seq
    cos_t, sin_t = _rope_tables(seq)

    bf = lambda w: w.astype(BF16)
    wgate_p = jnp.concatenate(
        [gla_w_gate, jnp.zeros((DEPTH, GLR_PAD - GLA_GATE_RANK, GLA_K_W), F32)], axis=1).astype(BF16)
    head_id = jnp.arange(SWA_Q_W, dtype=jnp.int32) // SWA_HEAD_DIM
    head_ones = (head_id[:, None] == head_id[None, :]).astype(BF16)
    reps = LANES // SWA_HEAD_DIM
    q_gain = jnp.tile(swa_q_norm, (1, reps))
    k_gain = jnp.tile(swa_k_norm, (1, reps))
    row = lambda g, l: g[l:l + 1]

    w_in_t = jnp.transpose(w_in, (0, 2, 1))
    ffn1_w = (ffn1_w_gate, ffn1_w_up, ffn1_w_down)
    ffn2_w = (ffn2_w_gate, ffn2_w_up, ffn2_w_down)
    mix_w = (w_proj_a, w_proj_b, w_out)
    at = lambda ws, l: [(w, l) for w in ws]

    f1 = tuple(bf(w[0]) for w in ffn1_w)
    h = x.reshape(n, D_MODEL)
    for l in range(DEPTH):
        convs = [(w_in_t, l)] + at(ffn2_w, l) + (at(mix_w, 0) if l == 0 else [])
        h, conv = _ffn(h, row(ffn1_norm, l), *f1, convs=convs)
        w_t, f2 = conv[0], conv[1:4]
        if l == 0:
            wa, wb, wo = conv[4:]
        qa, ka, va, qe, qin, kin, kst, vb, rb, ga, gb, dec = _inproj(
            h, row(mix_norm, l), w_t, cos_t, sin_t, row(q_gain, l), row(k_gain, l), wgate_p[l],
            row(gla_gate_bias, l), head_ones, seq)
        oa = _swa(qa, ka, va, swa_sinks[l], batch, seq)
        ob = _gla(qe, qin, kin, kst, vb, dec, batch, seq)
        h = _outproj(h, oa, ob, rb, ga, gb, row(gla_out_norm, l), wa, wb, wo)
        convs = [] if l + 1 == DEPTH else at(ffn1_w, l + 1) + at(mix_w, l + 1)
        h, conv = _ffn(h, row(ffn2_norm, l), *f2, convs=convs)
        if convs:
            f1, (wa, wb, wo) = conv[:3], conv[3:]
    return h.reshape(batch, seq, D_MODEL)
```

```python
import jax
import jax.numpy as jnp
from jax import lax
from jax.experimental import pallas as pl
from jax.experimental.pallas import tpu as pltpu

F32 = jnp.float32
BF16 = jnp.bfloat16

D_MODEL = 1024
DEPTH = 4
D_FF = 2816
EPS = 1e-6

SWA_HEADS = 8
SWA_KV_HEADS = 2
SWA_GROUP = SWA_HEADS // SWA_KV_HEADS
SWA_HEAD_DIM = 64
SWA_BLOCK = 128
ROPE_THETA = 10000.0
SWA_Q_W = SWA_HEADS * SWA_HEAD_DIM
SWA_KV_W = SWA_KV_HEADS * SWA_HEAD_DIM

GLA_HEADS = 4
GLA_DK = 128
GLA_DV = 256
GLA_GATE_RANK = 16
GLA_TAU = 16.0
GLA_K_W = GLA_HEADS * GLA_DK
GLA_V_W = GLA_HEADS * GLA_DV

IN_SPLITS = (SWA_Q_W, SWA_KV_W, SWA_KV_W, GLA_K_W, GLA_K_W, GLA_V_W, GLA_V_W, GLA_GATE_RANK, D_MODEL, D_MODEL)

LANES = 128
GLR_PAD = LANES
IN_COLS = sum(IN_SPLITS)
IN_OFF = tuple(sum(IN_SPLITS[:j]) for j in range(len(IN_SPLITS)))
(P_QA, P_KA, P_VA, P_QB, P_KB, P_VB, P_RB, P_GLR, P_GA, P_GB) = range(len(IN_SPLITS))

ROW_TILE = 512
GLA_CHUNK = 128
CHUNKS_PER_TILE = ROW_TILE // GLA_CHUNK
VMEM_LIMIT = 56 * 1024 * 1024
GLA_MIN_CHUNK_DECAY = 8.7e-27


def _rms(x, gain):
    ms = jnp.mean(x * x, axis=-1, keepdims=True)
    return x * lax.rsqrt(ms + EPS) * gain


def _dot(a, b):
    return jnp.dot(a, b, preferred_element_type=F32)


def _dot_nt(a, b):
    return lax.dot_general(a, b, (((1,), (1,)), ((), ())), preferred_element_type=F32)


def _dot_tn(a, b):
    return lax.dot_general(a, b, (((0,), (0,)), ((), ())), preferred_element_type=F32)


def _hi_lo(x):
    hi = x.astype(BF16)
    return hi, (x - hi.astype(F32)).astype(BF16)


def _resident(shape, index_map):
    return pl.BlockSpec(shape, index_map, pipeline_mode=pl.Buffered(1))


def _params(n_axes):
    return pltpu.CompilerParams(dimension_semantics=("arbitrary",) * n_axes,
                                vmem_limit_bytes=VMEM_LIMIT)


BF16_SUBLANES = 16


def _dense_call(body, name, n, in_specs, args, out_specs, out_shapes, convs=None):
    nsteps = n // ROW_TILE
    n_in, n_out, n_conv = len(in_specs), len(out_specs), len(convs or ())
    in_specs, out_specs, out_shapes, args = list(in_specs), list(out_specs), list(out_shapes), list(args)
    for w, layer in convs or ():
        _, rows, cols = w.shape
        nblk = max(b for b in range(1, nsteps + 1) if rows % (b * BF16_SUBLANES) == 0)
        last = nblk - 1
        in_specs.append(pl.BlockSpec((None, rows // nblk, cols),
                                     lambda i, layer=layer, last=last: (layer, jnp.minimum(i, last), 0)))
        out_specs.append(pl.BlockSpec((rows // nblk, cols), lambda i, last=last: (jnp.minimum(i, last), 0)))
        out_shapes.append(jax.ShapeDtypeStruct((rows, cols), BF16))
        args.append(w)

    def wrapped(*refs):
        ins, srcs = refs[:n_in], refs[n_in:n_in + n_conv]
        outs, dsts = refs[n_in + n_conv:n_in + n_conv + n_out], refs[n_in + n_conv + n_out:]

        def cast_weights():
            for s, d in zip(srcs, dsts):
                d[...] = s[...].astype(BF16)

        if convs is None:
            body(*ins, *outs)
        else:
            body(cast_weights, *ins, *outs)

    res = pl.pallas_call(
        wrapped, grid=(nsteps,), in_specs=in_specs, out_specs=out_specs, out_shape=out_shapes,
        compiler_params=_params(1), name=name)(*args)
    return res[:n_out], res[n_out:]


def _whole(shape):
    return _resident(shape, lambda i: (0,) * len(shape))


def _ffn_body(cast_weights, x_ref, gain_ref, wg_ref, wu_ref, wd_ref, o_ref):
    x = x_ref[...]
    h = _rms(x, gain_ref[...]).astype(BF16)
    g = _dot(h, wg_ref[...])
    cast_weights()
    u = _dot(h, wu_ref[...])
    a = (g * jax.nn.sigmoid(g) * u).astype(BF16)
    o_ref[...] = x + 0.5 * _dot(a, wd_ref[...])


def _ffn(x, gain, wg, wu, wd, convs=()):
    n = x.shape[0]
    row = pl.BlockSpec((ROW_TILE, D_MODEL), lambda i: (i, 0))
    (out,), converted = _dense_call(
        _ffn_body, "ffn", n,
        [row, _whole((1, D_MODEL)), _whole((D_MODEL, D_FF)), _whole((D_MODEL, D_FF)), _whole((D_FF, D_MODEL))],
        (x, gain, wg, wu, wd), [row], [jax.ShapeDtypeStruct((n, D_MODEL), F32)], convs)
    return out, converted


def _head_norm_rope(x, ssq, gain, cos, sin_signed, lane):
    xn = x * lax.rsqrt(ssq * (1.0 / SWA_HEAD_DIM) + EPS) * gain
    half = SWA_HEAD_DIM // 2
    partner = jnp.where((lane % SWA_HEAD_DIM) < half,
                        pltpu.roll(xn, LANES - half, 1), pltpu.roll(xn, half, 1))
    return xn * cos + partner * sin_signed


def _inproj_body(x_ref, gain_ref, w_ref, cos_ref, sin_ref, qg_ref, kg_ref, wgate_ref, bias_ref,
                 ones_ref, qa_ref, ka_ref, va_ref, qe_ref, qin_ref, kin_ref, kst_ref, vb_ref, rb_ref,
                 ga_ref, gb_ref, qraw_ref, kraw_ref, dec_ref, bcum_ref):
    h = _rms(x_ref[...], gain_ref[...]).astype(BF16)
    half = D_MODEL // 2
    lane = lax.broadcasted_iota(jnp.int32, (1, LANES), 1)

    def proj(which, lo=0, width=None):
        off = IN_OFF[which] + lo
        width = IN_SPLITS[which] if width is None else width
        return _dot_nt(h, w_ref[off:off + width, :])

    narrow = proj(P_QA, 0, IN_OFF[P_QB])
    tail = proj(P_RB, half, half + LANES)
    glr = jnp.where(lane < GLA_GATE_RANK, tail[:, half:], 0.0).astype(BF16)
    logits = _dot(glr, wgate_ref[...]) + bias_ref[...]
    r = tail[:, :half]
    rb_ref[:, half:] = (r * jax.nn.sigmoid(r)).astype(BF16)
    q = proj(P_QB) * GLA_DK ** -0.5
    la = (jnp.minimum(logits, 0.0) - jnp.log1p(jnp.exp(-jnp.abs(logits)))) * (1.0 / GLA_TAU)
    la_hi, la_lo = _hi_lo(la)
    k = proj(P_KB)
    qraw_ref[...] = q.astype(BF16)
    kraw_ref[...] = k.astype(BF16)

    c = GLA_CHUNK
    t_i = lax.broadcasted_iota(jnp.int32, (c, c), 0)
    s_i = lax.broadcasted_iota(jnp.int32, (c, c), 1)
    tril = (s_i <= t_i).astype(BF16)
    cums = []
    for ci in range(CHUNKS_PER_TILE):
        rows = slice(ci * c, (ci + 1) * c)
        cums.append(_dot(tril, la_hi[rows]) + _dot(tril, la_lo[rows]))

    def decay_products(ci):
        rows = slice(ci * c, (ci + 1) * c)
        b = cums[ci]
        b_last = b[c - 1:c, :]
        b_mid = b[c // 2 - 1:c // 2, :]
        qe_ref[rows, :] = (q[rows] * jnp.exp(b)).astype(BF16)
        qin_ref[rows, :] = (q[rows] * jnp.exp(b - b_mid)).astype(BF16)
        kin_ref[rows, :] = (k[rows] * jnp.exp(b_mid - b)).astype(BF16)
        kst_ref[rows, :] = (k[rows] * jnp.exp(b_last - b)).astype(BF16)
        dec_ref[0, ci:ci + 1, :] = jnp.exp(b_last)
        bcum_ref[rows, :] = b

    cos, sin = cos_ref[...], sin_ref[...]
    qa = narrow[:, :SWA_Q_W]
    ssq_q = _dot((qa * qa).astype(BF16), ones_ref[...])

    def rope_q(col):
        cs = slice(col * LANES, (col + 1) * LANES)
        qc = _head_norm_rope(qa[:, cs], ssq_q[:, cs], qg_ref[...], cos, sin, lane)
        qa_ref[:, cs] = (qc * SWA_HEAD_DIM ** -0.5).astype(BF16)

    vb_ref[:, :half] = proj(P_VB, 0, half).astype(BF16)
    decay_products(0)
    vb_ref[:, half:] = proj(P_VB, half, half).astype(BF16)
    decay_products(1)
    r = proj(P_RB, 0, half)
    decay_products(2)
    rb_ref[:, :half] = (r * jax.nn.sigmoid(r)).astype(BF16)
    decay_products(3)
    for col, (which, ref, lo) in enumerate(((P_GA, ga_ref, 0), (P_GA, ga_ref, half),
                                             (P_GB, gb_ref, 0), (P_GB, gb_ref, half))):
        ref[:, lo:lo + half] = jax.nn.sigmoid(proj(which, lo, half)).astype(BF16)
        rope_q(col)
    ka = narrow[:, IN_OFF[P_KA]:IN_OFF[P_VA]]
    ssq_k = _dot((ka * ka).astype(BF16), ones_ref[0:SWA_KV_W, 0:SWA_KV_W])
    ka_ref[...] = _head_norm_rope(ka, ssq_k, kg_ref[...], cos, sin, lane).astype(BF16)
    va_ref[...] = narrow[:, IN_OFF[P_VA]:IN_OFF[P_QB]].astype(BF16)


def _inproj(x, gain, w, cos, sin, q_gain, k_gain, wgate, bias, head_ones, seq):
    n = x.shape[0]
    nt = n // ROW_TILE
    tiles_per_seq = seq // ROW_TILE
    row = lambda i: (i, 0)
    pos = lambda i: (i % tiles_per_seq, 0)
    widths = (SWA_Q_W, SWA_KV_W, SWA_KV_W, GLA_K_W, GLA_K_W, GLA_K_W, GLA_K_W,
              GLA_V_W, GLA_V_W, D_MODEL, D_MODEL, GLA_K_W, GLA_K_W)
    outs, _ = _dense_call(
        _inproj_body, "inproj", n,
        [pl.BlockSpec((ROW_TILE, D_MODEL), row), _whole((1, D_MODEL)), _whole((IN_COLS, D_MODEL)),
         pl.BlockSpec((ROW_TILE, LANES), pos), pl.BlockSpec((ROW_TILE, LANES), pos),
         _whole((1, LANES)), _whole((1, LANES)), _whole((GLR_PAD, GLA_K_W)), _whole((1, GLA_K_W)),
         _whole((SWA_Q_W, SWA_Q_W))],
        (x, gain, w, cos, sin, q_gain, k_gain, wgate, bias, head_ones),
        [pl.BlockSpec((ROW_TILE, width), row) for width in widths]
        + [pl.BlockSpec((1, CHUNKS_PER_TILE, GLA_K_W), lambda i: (i, 0, 0)),
           pl.BlockSpec((ROW_TILE, GLA_K_W), row)],
        [jax.ShapeDtypeStruct((n, width), BF16) for width in widths]
        + [jax.ShapeDtypeStruct((nt, CHUNKS_PER_TILE, GLA_K_W), F32),
           jax.ShapeDtypeStruct((n, GLA_K_W), F32)])
    return outs


def _swa_body(sinks_ref, q_ref, k_ref, v_ref, o_ref, kbuf, vbuf):
    i = pl.program_id(1)
    tq = q_ref.shape[0]
    nblk = tq // SWA_BLOCK
    lane = lax.broadcasted_iota(jnp.int32, (1, LANES), 1)
    first_half = lane < SWA_HEAD_DIM

    @pl.when(i == 0)
    def _():
        kbuf[0:SWA_BLOCK, :] = jnp.zeros((SWA_BLOCK, LANES), BF16)
        vbuf[0:SWA_BLOCK, :] = jnp.zeros((SWA_BLOCK, LANES), BF16)

    kbuf[SWA_BLOCK:, :] = k_ref[...]
    vbuf[SWA_BLOCK:, :] = v_ref[...]
    kall = kbuf[...]
    vall = vbuf[...]
    kswap = pltpu.roll(kall, SWA_HEAD_DIM, 1)
    vswap = pltpu.roll(vall, SWA_HEAD_DIM, 1)
    kdup = [jnp.where(first_half, kall, kswap), jnp.where(first_half, kswap, kall)]
    vdup = [jnp.where(first_half, vall, vswap), jnp.where(first_half, vswap, vall)]

    rows = SWA_GROUP * SWA_BLOCK
    t_idx = lax.broadcasted_iota(jnp.int32, (rows, SWA_BLOCK), 0) % SWA_BLOCK
    c_idx = lax.broadcasted_iota(jnp.int32, (rows, SWA_BLOCK), 1)
    use_cur = c_idx <= t_idx
    row_head = lax.broadcasted_iota(jnp.int32, (rows, 1), 0) // SWA_BLOCK

    for kvh in range(SWA_KV_HEADS):
        sink = jnp.zeros((rows, 1), F32)
        for g in range(SWA_GROUP):
            sink = jnp.where(row_head == g, sinks_ref[kvh * SWA_GROUP + g], sink)
        for blk in range(nblk):
            r0 = blk * SWA_BLOCK
            parts = []
            for g in range(SWA_GROUP):
                head = kvh * SWA_GROUP + g
                qc = q_ref[r0:r0 + SWA_BLOCK, (head // 2) * LANES:(head // 2 + 1) * LANES]
                keep = first_half if head % 2 == 0 else jnp.logical_not(first_half)
                parts.append(jnp.where(keep, qc, jnp.zeros_like(qc)))
            qs = jnp.concatenate(parts, axis=0)
            keys = kdup[kvh][r0:r0 + 2 * SWA_BLOCK, :]
            vals = vdup[kvh][r0:r0 + 2 * SWA_BLOCK, :]
            s2 = _dot_nt(qs, keys)
            s_prev = s2[:, :SWA_BLOCK]
            if blk == 0:
                s_prev = s_prev + jnp.where(i > 0, 0.0, -jnp.inf)
            s = jnp.where(use_cur, s2[:, SWA_BLOCK:], s_prev)
            m = jnp.maximum(jnp.max(s, axis=-1, keepdims=True), sink)
            p = jnp.exp(s - m)
            denom = jnp.sum(p, axis=-1, keepdims=True) + jnp.exp(sink - m)
            p2 = jnp.concatenate([jnp.where(use_cur, 0.0, p), jnp.where(use_cur, p, 0.0)], axis=1)
            o = _dot(p2.astype(BF16), vals) / denom
            for pair in range(SWA_GROUP // 2):
                a = o[(2 * pair) * SWA_BLOCK:(2 * pair + 1) * SWA_BLOCK, :]
                b = o[(2 * pair + 1) * SWA_BLOCK:(2 * pair + 2) * SWA_BLOCK, :]
                col = (kvh * SWA_GROUP) // 2 + pair
                o_ref[r0:r0 + SWA_BLOCK, col * LANES:(col + 1) * LANES] = (
                    jnp.where(first_half, a, b).astype(o_ref.dtype))

    kbuf[0:SWA_BLOCK, :] = kbuf[tq:tq + SWA_BLOCK, :]
    vbuf[0:SWA_BLOCK, :] = vbuf[tq:tq + SWA_BLOCK, :]


def _swa(qa, ka, va, sinks, batch, seq):
    nt = seq // ROW_TILE
    tok = lambda b, i, *_: (b * nt + i, 0)
    return pl.pallas_call(
        _swa_body,
        grid_spec=pltpu.PrefetchScalarGridSpec(
            num_scalar_prefetch=1,
            grid=(batch, nt),
            in_specs=[
                pl.BlockSpec((ROW_TILE, SWA_Q_W), tok),
                pl.BlockSpec((ROW_TILE, SWA_KV_W), tok),
                pl.BlockSpec((ROW_TILE, SWA_KV_W), tok),
            ],
            out_specs=pl.BlockSpec((ROW_TILE, SWA_Q_W), tok),
            scratch_shapes=[pltpu.VMEM((ROW_TILE + SWA_BLOCK, LANES), BF16),
                            pltpu.VMEM((ROW_TILE + SWA_BLOCK, LANES), BF16)],
        ),
        out_shape=jax.ShapeDtypeStruct((batch * seq, SWA_Q_W), BF16),
        compiler_params=_params(2),
        name="swa",
    )(sinks, qa, ka, va)


def _gla_body(qe_ref, qin_ref, kin_ref, kst_ref, v_ref, dec_ref, o_ref, s_ref):
    i = pl.program_id(1)
    c = GLA_CHUNK

    @pl.when(i == 0)
    def _():
        s_ref[...] = jnp.zeros(s_ref.shape, F32)

    t_i = lax.broadcasted_iota(jnp.int32, (c, c), 0)
    s_i = lax.broadcasted_iota(jnp.int32, (c, c), 1)
    causal = s_i <= t_i

    for h in range(GLA_HEADS):
        ks = slice(h * GLA_DK, (h + 1) * GLA_DK)
        vs = slice(h * GLA_DV, (h + 1) * GLA_DV)
        state = s_ref[h]
        for ci in range(CHUNKS_PER_TILE):
            rows = slice(ci * c, (ci + 1) * c)
            v = v_ref[rows, vs]
            attn = jnp.where(causal, _dot_nt(qin_ref[rows, ks], kin_ref[rows, ks]), 0.0)
            lhs = jnp.concatenate([qe_ref[rows, ks], attn.astype(BF16)], axis=1)
            rhs = jnp.concatenate([state.astype(BF16), v], axis=0)
            o_ref[rows, vs] = _dot(lhs, rhs).astype(o_ref.dtype)
            state = (_decay_columns(dec_ref[0, ci:ci + 1, ks]) * state
                     + _dot_tn(kst_ref[rows, ks], v))
        s_ref[h] = state


def _decay_columns(decay_row):
    cols = jnp.transpose(jnp.broadcast_to(decay_row, (GLA_CHUNK, GLA_DK)))
    return jnp.concatenate([cols] * (GLA_DV // GLA_CHUNK), axis=1)


def _gla_exact_body(q_ref, k_ref, v_ref, b_ref, o_ref, s_ref, kf_ref, bf_ref):
    i = pl.program_id(1)
    c = GLA_CHUNK

    @pl.when(i == 0)
    def _():
        s_ref[...] = jnp.zeros(s_ref.shape, F32)

    t_col = lax.broadcasted_iota(jnp.int32, (c, 1), 0)
    s_lane = lax.broadcasted_iota(jnp.int32, (1, c), 1)

    for h in range(GLA_HEADS):
        ks = slice(h * GLA_DK, (h + 1) * GLA_DK)
        vs = slice(h * GLA_DV, (h + 1) * GLA_DV)

        def chunk_step(ci, carry, h=h, ks=ks, vs=vs):
            rows = pl.ds(pl.multiple_of(ci * c, c), c)
            b = b_ref[rows, ks]
            q = q_ref[rows, ks].astype(F32)
            k = k_ref[rows, ks].astype(F32)
            v = v_ref[rows, vs]
            kf_ref[...] = k
            bf_ref[...] = b

            def key_column(s, attn):
                pair = jnp.exp(jnp.minimum(b - bf_ref[pl.ds(s, 1), :], 0.0))
                col = jnp.sum(q * pair * kf_ref[pl.ds(s, 1), :], axis=-1, keepdims=True)
                col = jnp.where(t_col >= s, col, 0.0)
                return attn + col * (s_lane == s).astype(F32)

            attn = lax.fori_loop(0, c, key_column, jnp.zeros((c, c), F32))
            b_last = b[c - 1:c, :]
            state = s_ref[h]
            o = _dot((q * jnp.exp(b)).astype(BF16), state.astype(BF16)) + _dot(attn.astype(BF16), v)
            o_ref[rows, vs] = o.astype(o_ref.dtype)
            s_ref[h] = (_decay_columns(jnp.exp(b_last)) * state
                        + _dot_tn((k * jnp.exp(b_last - b)).astype(BF16), v))
            return carry

        lax.fori_loop(0, CHUNKS_PER_TILE, chunk_step, 0)


def _gla_exact(q, k, vb, bcum, batch, seq):
    nt = seq // ROW_TILE
    tok = lambda b, i: (b * nt + i, 0)
    kspec = pl.BlockSpec((ROW_TILE, GLA_K_W), tok)
    vspec = pl.BlockSpec((ROW_TILE, GLA_V_W), tok)
    return pl.pallas_call(
        _gla_exact_body,
        grid=(batch, nt),
        in_specs=[kspec, kspec, vspec, kspec],
        out_specs=vspec,
        out_shape=jax.ShapeDtypeStruct((batch * seq, GLA_V_W), BF16),
        scratch_shapes=[pltpu.VMEM((GLA_HEADS, GLA_DK, GLA_DV), F32),
                        pltpu.VMEM((GLA_CHUNK, GLA_DK), F32), pltpu.VMEM((GLA_CHUNK, GLA_DK), F32)],
        compiler_params=_params(2),
        name="gla_exact",
    )(q, k, vb, bcum)


def _gla(qe, qin, kin, kst, vb, dec, batch, seq):
    nt = seq // ROW_TILE
    tok = lambda b, i: (b * nt + i, 0)
    kspec = pl.BlockSpec((ROW_TILE, GLA_K_W), tok)
    vspec = pl.BlockSpec((ROW_TILE, GLA_V_W), tok)
    return pl.pallas_call(
        _gla_body,
        grid=(batch, nt),
        in_specs=[kspec, kspec, kspec, kspec, vspec,
                  pl.BlockSpec((1, CHUNKS_PER_TILE, GLA_K_W), lambda b, i: (b * nt + i, 0, 0))],
        out_specs=vspec,
        out_shape=jax.ShapeDtypeStruct((batch * seq, GLA_V_W), BF16),
        scratch_shapes=[pltpu.VMEM((GLA_HEADS, GLA_DK, GLA_DV), F32)],
        compiler_params=_params(2),
        name="gla",
    )(qe, qin, kin, kst, vb, dec)


def _outproj_body(x_ref, oa_ref, ob_ref, rb_ref, ga_ref, gb_ref, og_ref, wa_ref, wb_ref, wo_ref, o_ref):
    quarter = D_MODEL // GLA_HEADS
    ya = []
    gated = []
    for h in range(GLA_HEADS):
        vs = slice(h * GLA_DV, (h + 1) * GLA_DV)
        ya.append(_dot(oa_ref[...], wa_ref[:, h * quarter:(h + 1) * quarter]))
        on = _rms(ob_ref[:, vs].astype(F32), og_ref[:, vs])
        gated.append((on * rb_ref[:, vs].astype(F32)).astype(BF16))
    gated = jnp.concatenate(gated, axis=1)
    merged = []
    for j in range(GLA_HEADS):
        cs = slice(j * quarter, (j + 1) * quarter)
        yb = _dot(gated, wb_ref[:, cs])
        merged.append((ga_ref[:, cs].astype(F32) * ya[j] + gb_ref[:, cs].astype(F32) * yb).astype(BF16))
    o_ref[...] = x_ref[...] + _dot(jnp.concatenate(merged, axis=1), wo_ref[...])


def _outproj(x, oa, ob, rb, ga, gb, out_gain, wa, wb, wo):
    n = x.shape[0]
    spec = lambda width: pl.BlockSpec((ROW_TILE, width), lambda i: (i, 0))
    (out,), _ = _dense_call(
        _outproj_body, "outproj", n,
        [spec(D_MODEL), spec(SWA_Q_W), spec(GLA_V_W), spec(GLA_V_W), spec(D_MODEL), spec(D_MODEL),
         _whole((1, GLA_V_W)), _whole((SWA_Q_W, D_MODEL)), _whole((GLA_V_W, D_MODEL)),
         _whole((D_MODEL, D_MODEL))],
        (x, oa, ob, rb, ga, gb, out_gain, wa, wb, wo),
        [spec(D_MODEL)], [jax.ShapeDtypeStruct((n, D_MODEL), F32)])
    return out


def _rope_tables(seq):
    inv_freq = ROPE_THETA ** (-jnp.arange(0, SWA_HEAD_DIM, 2, dtype=F32) / SWA_HEAD_DIM)
    ang = jnp.arange(seq, dtype=F32)[:, None] * inv_freq[None, :]
    cos, sin = jnp.cos(ang), jnp.sin(ang)
    reps = LANES // SWA_HEAD_DIM
    cos_t = jnp.tile(jnp.concatenate([cos, cos], axis=-1), (1, reps))
    sin_t = jnp.tile(jnp.concatenate([-sin, sin], axis=-1), (1, reps))
    return cos_t, sin_t


def kernel(x, ffn1_norm, ffn1_w_gate, ffn1_w_up, ffn1_w_down, mix_norm, w_in, swa_q_norm, swa_k_norm, swa_sinks, gla_w_gate, gla_gate_bias, gla_out_norm, w_proj_a, w_proj_b, w_out, ffn2_norm, ffn2_w_gate, ffn2_w_up, ffn2_w_down):
    batch, seq, d = x.shape
    assert d == D_MODEL and seq % ROW_TILE == 0
    n = batch * seq
    cos_t, sin_t = _rope_tables(seq)

    bf = lambda w: w.astype(BF16)
    wgate_p = jnp.concatenate(
        [gla_w_gate, jnp.zeros((DEPTH, GLR_PAD - GLA_GATE_RANK, GLA_K_W), F32)], axis=1).astype(BF16)
    head_id = jnp.arange(SWA_Q_W, dtype=jnp.int32) // SWA_HEAD_DIM
    head_ones = (head_id[:, None] == head_id[None, :]).astype(BF16)
    reps = LANES // SWA_HEAD_DIM
    q_gain = jnp.tile(swa_q_norm, (1, reps))
    k_gain = jnp.tile(swa_k_norm, (1, reps))
    row = lambda g, l: g[l:l + 1]

    w_in_t = jnp.transpose(w_in, (0, 2, 1))
    ffn1_w = (ffn1_w_gate, ffn1_w_up, ffn1_w_down)
    ffn2_w = (ffn2_w_gate, ffn2_w_up, ffn2_w_down)
    mix_w = (w_proj_a, w_proj_b, w_out)
    at = lambda ws, l: [(w, l) for w in ws]

    f1 = tuple(bf(w[0]) for w in ffn1_w)
    h = x.reshape(n, D_MODEL)
    for l in range(DEPTH):
        convs = [(w_in_t, l)] + at(ffn2_w, l) + (at(mix_w, 0) if l == 0 else [])
        h, conv = _ffn(h, row(ffn1_norm, l), *f1, convs=convs)
        w_t, f2 = conv[0], conv[1:4]
        if l == 0:
            wa, wb, wo = conv[4:]
        qa, ka, va, qe, qin, kin, kst, vb, rb, ga, gb, qraw, kraw, dec, bcum = _inproj(
            h, row(mix_norm, l), w_t, cos_t, sin_t, row(q_gain, l), row(k_gain, l), wgate_p[l],
            row(gla_gate_bias, l), head_ones, seq)
        oa = _swa(qa, ka, va, swa_sinks[l], batch, seq)
        ob = lax.cond(
            jnp.min(dec) >= GLA_MIN_CHUNK_DECAY,
            lambda *ops: _gla(*ops[:6], batch, seq),
            lambda *ops: _gla_exact(ops[6], ops[7], ops[4], ops[8], batch, seq),
            qe, qin, kin, kst, vb, dec, qraw, kraw, bcum)
        h = _outproj(h, oa, ob, rb, ga, gb, row(gla_out_norm, l), wa, wb, wo)
        convs = [] if l + 1 == DEPTH else at(ffn1_w, l + 1) + at(mix_w, l + 1)
        h, conv = _ffn(h, row(ffn2_norm, l), *f2, convs=convs)
        if convs:
            f1, (wa, wb, wo) = conv[:3], conv[3:]
    return h.reshape(batch, seq, D_MODEL)
```

```python
import jax
import jax.numpy as jnp
from jax import lax
from jax.experimental import pallas as pl
from jax.experimental.pallas import tpu as pltpu

F32 = jnp.float32
BF16 = jnp.bfloat16

D_MODEL = 1024
DEPTH = 4
D_FF = 2816
EPS = 1e-6

SWA_HEADS = 8
SWA_KV_HEADS = 2
SWA_GROUP = SWA_HEADS // SWA_KV_HEADS
SWA_HEAD_DIM = 64
SWA_BLOCK = 128
ROPE_THETA = 10000.0
SWA_Q_W = SWA_HEADS * SWA_HEAD_DIM
SWA_KV_W = SWA_KV_HEADS * SWA_HEAD_DIM

GLA_HEADS = 4
GLA_DK = 128
GLA_DV = 256
GLA_GATE_RANK = 16
GLA_TAU = 16.0
GLA_K_W = GLA_HEADS * GLA_DK
GLA_V_W = GLA_HEADS * GLA_DV

IN_SPLITS = (SWA_Q_W, SWA_KV_W, SWA_KV_W, GLA_K_W, GLA_K_W, GLA_V_W, GLA_V_W, GLA_GATE_RANK, D_MODEL, D_MODEL)

LANES = 128
GLR_PAD = LANES
IN_COLS = sum(IN_SPLITS)
IN_OFF = tuple(sum(IN_SPLITS[:j]) for j in range(len(IN_SPLITS)))
(P_QA, P_KA, P_VA, P_QB, P_KB, P_VB, P_RB, P_GLR, P_GA, P_GB) = range(len(IN_SPLITS))

ROW_TILE = 512
GLA_CHUNK = 128
CHUNKS_PER_TILE = ROW_TILE // GLA_CHUNK
VMEM_LIMIT = 56 * 1024 * 1024
GLA_MIN_CHUNK_DECAY = 8.7e-27


def _rms(x, gain):
    ms = jnp.mean(x * x, axis=-1, keepdims=True)
    return x * lax.rsqrt(ms + EPS) * gain


def _sigmoid(x):
    return 0.5 * jnp.tanh(0.5 * x) + 0.5


def _dot(a, b):
    return jnp.dot(a, b, preferred_element_type=F32)


def _dot_nt(a, b):
    return lax.dot_general(a, b, (((1,), (1,)), ((), ())), preferred_element_type=F32)


def _dot_tn(a, b):
    return lax.dot_general(a, b, (((0,), (0,)), ((), ())), preferred_element_type=F32)


def _hi_lo(x):
    hi = x.astype(BF16)
    return hi, (x - hi.astype(F32)).astype(BF16)


def _resident(shape, index_map):
    return pl.BlockSpec(shape, index_map, pipeline_mode=pl.Buffered(1))


def _params(n_axes):
    return pltpu.CompilerParams(dimension_semantics=("arbitrary",) * n_axes,
                                vmem_limit_bytes=VMEM_LIMIT)


BF16_SUBLANES = 16


def _dense_call(body, name, n, in_specs, args, out_specs, out_shapes, convs=None):
    nsteps = n // ROW_TILE
    n_in, n_out, n_conv = len(in_specs), len(out_specs), len(convs or ())
    in_specs, out_specs, out_shapes, args = list(in_specs), list(out_specs), list(out_shapes), list(args)
    for w, layer in convs or ():
        _, rows, cols = w.shape
        nblk = max(b for b in range(1, nsteps + 1) if rows % (b * BF16_SUBLANES) == 0)
        last = nblk - 1
        in_specs.append(pl.BlockSpec((None, rows // nblk, cols),
                                     lambda i, layer=layer, last=last: (layer, jnp.minimum(i, last), 0)))
        out_specs.append(pl.BlockSpec((rows // nblk, cols), lambda i, last=last: (jnp.minimum(i, last), 0)))
        out_shapes.append(jax.ShapeDtypeStruct((rows, cols), BF16))
        args.append(w)

    def wrapped(*refs):
        ins, srcs = refs[:n_in], refs[n_in:n_in + n_conv]
        outs, dsts = refs[n_in + n_conv:n_in + n_conv + n_out], refs[n_in + n_conv + n_out:]

        def cast_weights():
            for s, d in zip(srcs, dsts):
                d[...] = s[...].astype(BF16)

        if convs is None:
            body(*ins, *outs)
        else:
            body(cast_weights, *ins, *outs)

    res = pl.pallas_call(
        wrapped, grid=(nsteps,), in_specs=in_specs, out_specs=out_specs, out_shape=out_shapes,
        compiler_params=_params(1), name=name)(*args)
    return res[:n_out], res[n_out:]


def _whole(shape):
    return _resident(shape, lambda i: (0,) * len(shape))


def _ffn_body(cast_weights, x_ref, gain_ref, wg_ref, wu_ref, wd_ref, o_ref):
    x = x_ref[...]
    h = _rms(x, gain_ref[...]).astype(BF16)
    g = _dot(h, wg_ref[...])
    cast_weights()
    u = _dot(h, wu_ref[...])
    a = (g * _sigmoid(g) * u).astype(BF16)
    o_ref[...] = x + 0.5 * _dot(a, wd_ref[...])


def _ffn(x, gain, wg, wu, wd, convs=()):
    n = x.shape[0]
    row = pl.BlockSpec((ROW_TILE, D_MODEL), lambda i: (i, 0))
    (out,), converted = _dense_call(
        _ffn_body, "ffn", n,
        [row, _whole((1, D_MODEL)), _whole((D_MODEL, D_FF)), _whole((D_MODEL, D_FF)), _whole((D_FF, D_MODEL))],
        (x, gain, wg, wu, wd), [row], [jax.ShapeDtypeStruct((n, D_MODEL), F32)], convs)
    return out, converted


def _head_norm_rope(x, ssq, gain, cos, sin_signed, lane):
    xn = x * lax.rsqrt(ssq * (1.0 / SWA_HEAD_DIM) + EPS) * gain
    half = SWA_HEAD_DIM // 2
    partner = jnp.where((lane % SWA_HEAD_DIM) < half,
                        pltpu.roll(xn, LANES - half, 1), pltpu.roll(xn, half, 1))
    return xn * cos + partner * sin_signed


def _inproj_body(x_ref, gain_ref, w_ref, cos_ref, sin_ref, qg_ref, kg_ref, wgate_ref, bias_ref,
                 ones_ref, qa_ref, ka_ref, va_ref, qe_ref, qin_ref, kin_ref, kst_ref, vb_ref, rb_ref,
                 ga_ref, gb_ref, qraw_ref, kraw_ref, dec_ref, bcum_ref):
    h = _rms(x_ref[...], gain_ref[...]).astype(BF16)
    half = D_MODEL // 2
    lane = lax.broadcasted_iota(jnp.int32, (1, LANES), 1)

    def proj(which, lo=0, width=None):
        off = IN_OFF[which] + lo
        width = IN_SPLITS[which] if width is None else width
        return _dot_nt(h, w_ref[off:off + width, :])

    narrow = proj(P_QA, 0, IN_OFF[P_QB])
    tail = proj(P_RB, half, half + LANES)
    glr = jnp.where(lane < GLA_GATE_RANK, tail[:, half:], 0.0).astype(BF16)
    logits = _dot(glr, wgate_ref[...]) + bias_ref[...]
    r = tail[:, :half]
    rb_ref[:, half:] = (r * _sigmoid(r)).astype(BF16)
    q = proj(P_QB) * GLA_DK ** -0.5
    la = (jnp.minimum(logits, 0.0) - jnp.log1p(jnp.exp(-jnp.abs(logits)))) * (1.0 / GLA_TAU)
    la_hi, la_lo = _hi_lo(la)
    k = proj(P_KB)
    qraw_ref[...] = q.astype(BF16)
    kraw_ref[...] = k.astype(BF16)

    c = GLA_CHUNK
    t_i = lax.broadcasted_iota(jnp.int32, (c, c), 0)
    s_i = lax.broadcasted_iota(jnp.int32, (c, c), 1)
    tril = (s_i <= t_i).astype(BF16)
    cums = []
    for ci in range(CHUNKS_PER_TILE):
        rows = slice(ci * c, (ci + 1) * c)
        cums.append(_dot(tril, la_hi[rows]) + _dot(tril, la_lo[rows]))

    def decay_products(ci):
        rows = slice(ci * c, (ci + 1) * c)
        b = cums[ci]
        b_last = b[c - 1:c, :]
        b_mid = b[c // 2 - 1:c // 2, :]
        q_in = q[rows] * jnp.exp(b - b_mid)
        k_in = k[rows] * jnp.exp(b_mid - b)
        qin_ref[rows, :] = q_in.astype(BF16)
        kin_ref[rows, :] = k_in.astype(BF16)
        qe_ref[rows, :] = (q_in * jnp.exp(b_mid)).astype(BF16)
        kst_ref[rows, :] = (k_in * jnp.exp(b_last - b_mid)).astype(BF16)
        dec_ref[0, ci:ci + 1, :] = jnp.exp(b_last)
        bcum_ref[rows, :] = b

    cos, sin = cos_ref[...], sin_ref[...]
    qa = narrow[:, :SWA_Q_W]
    ssq_q = _dot((qa * qa).astype(BF16), ones_ref[...])

    def rope_q(col):
        cs = slice(col * LANES, (col + 1) * LANES)
        qc = _head_norm_rope(qa[:, cs], ssq_q[:, cs], qg_ref[...], cos, sin, lane)
        qa_ref[:, cs] = (qc * SWA_HEAD_DIM ** -0.5).astype(BF16)

    vb_ref[:, :half] = proj(P_VB, 0, half).astype(BF16)
    decay_products(0)
    vb_ref[:, half:] = proj(P_VB, half, half).astype(BF16)
    decay_products(1)
    r = proj(P_RB, 0, half)
    decay_products(2)
    rb_ref[:, :half] = (r * _sigmoid(r)).astype(BF16)
    decay_products(3)
    for col, (which, ref, lo) in enumerate(((P_GA, ga_ref, 0), (P_GA, ga_ref, half),
                                             (P_GB, gb_ref, 0), (P_GB, gb_ref, half))):
        ref[:, lo:lo + half] = _sigmoid(proj(which, lo, half)).astype(BF16)
        rope_q(col)
    ka = narrow[:, IN_OFF[P_KA]:IN_OFF[P_VA]]
    ssq_k = _dot((ka * ka).astype(BF16), ones_ref[0:SWA_KV_W, 0:SWA_KV_W])
    ka_ref[...] = _head_norm_rope(ka, ssq_k, kg_ref[...], cos, sin, lane).astype(BF16)
    va_ref[...] = narrow[:, IN_OFF[P_VA]:IN_OFF[P_QB]].astype(BF16)


def _inproj(x, gain, w, cos, sin, q_gain, k_gain, wgate, bias, head_ones, seq):
    n = x.shape[0]
    nt = n // ROW_TILE
    tiles_per_seq = seq // ROW_TILE
    row = lambda i: (i, 0)
    pos = lambda i: (i % tiles_per_seq, 0)
    widths = (SWA_Q_W, SWA_KV_W, SWA_KV_W, GLA_K_W, GLA_K_W, GLA_K_W, GLA_K_W,
              GLA_V_W, GLA_V_W, D_MODEL, D_MODEL, GLA_K_W, GLA_K_W)
    outs, _ = _dense_call(
        _inproj_body, "inproj", n,
        [pl.BlockSpec((ROW_TILE, D_MODEL), row), _whole((1, D_MODEL)), _whole((IN_COLS, D_MODEL)),
         pl.BlockSpec((ROW_TILE, LANES), pos), pl.BlockSpec((ROW_TILE, LANES), pos),
         _whole((1, LANES)), _whole((1, LANES)), _whole((GLR_PAD, GLA_K_W)), _whole((1, GLA_K_W)),
         _whole((SWA_Q_W, SWA_Q_W))],
        (x, gain, w, cos, sin, q_gain, k_gain, wgate, bias, head_ones),
        [pl.BlockSpec((ROW_TILE, width), row) for width in widths]
        + [pl.BlockSpec((1, CHUNKS_PER_TILE, GLA_K_W), lambda i: (i, 0, 0)),
           pl.BlockSpec((ROW_TILE, GLA_K_W), row)],
        [jax.ShapeDtypeStruct((n, width), BF16) for width in widths]
        + [jax.ShapeDtypeStruct((nt, CHUNKS_PER_TILE, GLA_K_W), F32),
           jax.ShapeDtypeStruct((n, GLA_K_W), F32)])
    return outs


def _swa_setup(i, k_ref, v_ref, kbuf, vbuf):
    lane = lax.broadcasted_iota(jnp.int32, (1, LANES), 1)
    first_half = lane < SWA_HEAD_DIM

    @pl.when(i == 0)
    def _():
        kbuf[0:SWA_BLOCK, :] = jnp.zeros((SWA_BLOCK, LANES), BF16)
        vbuf[0:SWA_BLOCK, :] = jnp.zeros((SWA_BLOCK, LANES), BF16)

    kbuf[SWA_BLOCK:, :] = k_ref[...]
    vbuf[SWA_BLOCK:, :] = v_ref[...]
    kall = kbuf[...]
    vall = vbuf[...]
    kswap = pltpu.roll(kall, SWA_HEAD_DIM, 1)
    vswap = pltpu.roll(vall, SWA_HEAD_DIM, 1)
    kdup = [jnp.where(first_half, kall, kswap), jnp.where(first_half, kswap, kall)]
    vdup = [jnp.where(first_half, vall, vswap), jnp.where(first_half, vswap, vall)]

    rows = SWA_GROUP * SWA_BLOCK
    t_idx = lax.broadcasted_iota(jnp.int32, (rows, SWA_BLOCK), 0) % SWA_BLOCK
    c_idx = lax.broadcasted_iota(jnp.int32, (rows, SWA_BLOCK), 1)
    use_cur = c_idx <= t_idx
    row_head = lax.broadcasted_iota(jnp.int32, (rows, 1), 0) // SWA_BLOCK
    return first_half, kdup, vdup, use_cur, row_head


def _swa_scores(ctx, q_ref, kvh, blk):
    first_half, kdup = ctx[0], ctx[1]
    r0 = blk * SWA_BLOCK
    parts = []
    for g in range(SWA_GROUP):
        head = kvh * SWA_GROUP + g
        qc = q_ref[r0:r0 + SWA_BLOCK, (head // 2) * LANES:(head // 2 + 1) * LANES]
        keep = first_half if head % 2 == 0 else jnp.logical_not(first_half)
        parts.append(jnp.where(keep, qc, jnp.zeros_like(qc)))
    return _dot_nt(jnp.concatenate(parts, axis=0), kdup[kvh][r0:r0 + 2 * SWA_BLOCK, :])


def _swa_attend(ctx, i, sinks_ref, o_ref, s2, kvh, blk):
    first_half, _, vdup, use_cur, row_head = ctx
    rows = SWA_GROUP * SWA_BLOCK
    sink = jnp.zeros((rows, 1), F32)
    for g in range(SWA_GROUP):
        sink = jnp.where(row_head == g, sinks_ref[kvh * SWA_GROUP + g], sink)
    r0 = blk * SWA_BLOCK
    s_prev = s2[:, :SWA_BLOCK]
    if blk == 0:
        s_prev = s_prev + jnp.where(i > 0, 0.0, -jnp.inf)
    s = jnp.where(use_cur, s2[:, SWA_BLOCK:], s_prev)
    m = jnp.maximum(jnp.max(s, axis=-1, keepdims=True), sink)
    p = jnp.exp(s - m)
    denom = jnp.sum(p, axis=-1, keepdims=True) + jnp.exp(sink - m)
    p2 = jnp.concatenate([jnp.where(use_cur, 0.0, p), jnp.where(use_cur, p, 0.0)], axis=1)
    vals = vdup[kvh][r0:r0 + 2 * SWA_BLOCK, :]
    o = _dot(p2.astype(BF16), vals) / denom
    for pair in range(SWA_GROUP // 2):
        a = o[(2 * pair) * SWA_BLOCK:(2 * pair + 1) * SWA_BLOCK, :]
        b = o[(2 * pair + 1) * SWA_BLOCK:(2 * pair + 2) * SWA_BLOCK, :]
        col = (kvh * SWA_GROUP) // 2 + pair
        o_ref[r0:r0 + SWA_BLOCK, col * LANES:(col + 1) * LANES] = (
            jnp.where(first_half, a, b).astype(o_ref.dtype))


def _swa_carry(kbuf, vbuf):
    kbuf[0:SWA_BLOCK, :] = kbuf[ROW_TILE:ROW_TILE + SWA_BLOCK, :]
    vbuf[0:SWA_BLOCK, :] = vbuf[ROW_TILE:ROW_TILE + SWA_BLOCK, :]


def _decay_columns(decay_row):
    cols = jnp.transpose(jnp.broadcast_to(decay_row, (GLA_CHUNK, GLA_DK)))
    return jnp.concatenate([cols] * (GLA_DV // GLA_CHUNK), axis=1)


def _gla_body(qe_ref, qin_ref, kin_ref, kst_ref, v_ref, dec_ref, o_ref, s_ref):
    i = pl.program_id(1)
    c = GLA_CHUNK

    @pl.when(i == 0)
    def _():
        s_ref[...] = jnp.zeros(s_ref.shape, F32)

    t_i = lax.broadcasted_iota(jnp.int32, (c, c), 0)
    s_i = lax.broadcasted_iota(jnp.int32, (c, c), 1)
    causal = s_i <= t_i
    items = [(ci, h) for ci in range(CHUNKS_PER_TILE) for h in range(GLA_HEADS)]
    window = lambda ci, h: (slice(ci * c, (ci + 1) * c), slice(h * GLA_DK, (h + 1) * GLA_DK),
                            slice(h * GLA_DV, (h + 1) * GLA_DV))

    attn, update = {}, {}
    for ci, h in items:
        rows, ks, vs = window(ci, h)
        scores = _dot_nt(qin_ref[rows, ks], kin_ref[rows, ks])
        attn[ci, h] = jnp.where(causal, scores, 0.0).astype(BF16)
        update[ci, h] = _dot_tn(kst_ref[rows, ks], v_ref[rows, vs])

    states = [s_ref[h] for h in range(GLA_HEADS)]
    for ci, h in items:
        rows, ks, vs = window(ci, h)
        lhs = jnp.concatenate([qe_ref[rows, ks], attn[ci, h]], axis=1)
        rhs = jnp.concatenate([states[h].astype(BF16), v_ref[rows, vs]], axis=0)
        o_ref[rows, vs] = _dot(lhs, rhs).astype(o_ref.dtype)
        states[h] = _decay_columns(dec_ref[0, ci:ci + 1, ks]) * states[h] + update[ci, h]
    for h in range(GLA_HEADS):
        s_ref[h] = states[h]


def _gla(qe, qin, kin, kst, vb, dec, batch, seq):
    nt = seq // ROW_TILE
    tok = lambda b, i: (b * nt + i, 0)
    kspec = pl.BlockSpec((ROW_TILE, GLA_K_W), tok)
    vspec = pl.BlockSpec((ROW_TILE, GLA_V_W), tok)
    return pl.pallas_call(
        _gla_body,
        grid=(batch, nt),
        in_specs=[kspec, kspec, kspec, kspec, vspec,
                  pl.BlockSpec((1, CHUNKS_PER_TILE, GLA_K_W), lambda b, i: (b * nt + i, 0, 0))],
        out_specs=vspec,
        out_shape=jax.ShapeDtypeStruct((batch * seq, GLA_V_W), BF16),
        scratch_shapes=[pltpu.VMEM((GLA_HEADS, GLA_DK, GLA_DV), F32)],
        compiler_params=_params(2),
        name="gla",
    )(qe, qin, kin, kst, vb, dec)


def _swa_body(sinks_ref, q_ref, k_ref, v_ref, o_ref, kbuf, vbuf):
    i = pl.program_id(1)
    ctx = _swa_setup(i, k_ref, v_ref, kbuf, vbuf)
    items = [(kvh, blk) for kvh in range(SWA_KV_HEADS) for blk in range(ROW_TILE // SWA_BLOCK)]
    scores = [_swa_scores(ctx, q_ref, kvh, blk) for kvh, blk in items]
    for s2, (kvh, blk) in zip(scores, items):
        _swa_attend(ctx, i, sinks_ref, o_ref, s2, kvh, blk)
    _swa_carry(kbuf, vbuf)


def _swa(qa, ka, va, sinks, batch, seq):
    nt = seq // ROW_TILE
    tok = lambda b, i, *_: (b * nt + i, 0)
    return pl.pallas_call(
        _swa_body,
        grid_spec=pltpu.PrefetchScalarGridSpec(
            num_scalar_prefetch=1,
            grid=(batch, nt),
            in_specs=[
                pl.BlockSpec((ROW_TILE, SWA_Q_W), tok),
                pl.BlockSpec((ROW_TILE, SWA_KV_W), tok),
                pl.BlockSpec((ROW_TILE, SWA_KV_W), tok),
            ],
            out_specs=pl.BlockSpec((ROW_TILE, SWA_Q_W), tok),
            scratch_shapes=[pltpu.VMEM((ROW_TILE + SWA_BLOCK, LANES), BF16),
                            pltpu.VMEM((ROW_TILE + SWA_BLOCK, LANES), BF16)],
        ),
        out_shape=jax.ShapeDtypeStruct((batch * seq, SWA_Q_W), BF16),
        compiler_params=_params(2),
        name="swa",
    )(sinks, qa, ka, va)


def _gla_exact_body(q_ref, k_ref, v_ref, b_ref, o_ref, s_ref, kf_ref, bf_ref):
    i = pl.program_id(1)
    c = GLA_CHUNK

    @pl.when(i == 0)
    def _():
        s_ref[...] = jnp.zeros(s_ref.shape, F32)

    t_col = lax.broadcasted_iota(jnp.int32, (c, 1), 0)
    s_lane = lax.broadcasted_iota(jnp.int32, (1, c), 1)

    for h in range(GLA_HEADS):
        ks = slice(h * GLA_DK, (h + 1) * GLA_DK)
        vs = slice(h * GLA_DV, (h + 1) * GLA_DV)

        def chunk_step(ci, carry, h=h, ks=ks, vs=vs):
            rows = pl.ds(pl.multiple_of(ci * c, c), c)
            b = b_ref[rows, ks]
            q = q_ref[rows, ks].astype(F32)
            k = k_ref[rows, ks].astype(F32)
            v = v_ref[rows, vs]
            kf_ref[...] = k
            bf_ref[...] = b

            def key_column(s, attn):
                pair = jnp.exp(jnp.minimum(b - bf_ref[pl.ds(s, 1), :], 0.0))
                col = jnp.sum(q * pair * kf_ref[pl.ds(s, 1), :], axis=-1, keepdims=True)
                col = jnp.where(t_col >= s, col, 0.0)
                return attn + col * (s_lane == s).astype(F32)

            attn = lax.fori_loop(0, c, key_column, jnp.zeros((c, c), F32))
            b_last = b[c - 1:c, :]
            state = s_ref[h]
            o = _dot((q * jnp.exp(b)).astype(BF16), state.astype(BF16)) + _dot(attn.astype(BF16), v)
            o_ref[rows, vs] = o.astype(o_ref.dtype)
            s_ref[h] = (_decay_columns(jnp.exp(b_last)) * state
                        + _dot_tn((k * jnp.exp(b_last - b)).astype(BF16), v))
            return carry

        lax.fori_loop(0, CHUNKS_PER_TILE, chunk_step, 0)


def _gla_exact(q, k, vb, bcum, batch, seq):
    nt = seq // ROW_TILE
    tok = lambda b, i: (b * nt + i, 0)
    kspec = pl.BlockSpec((ROW_TILE, GLA_K_W), tok)
    vspec = pl.BlockSpec((ROW_TILE, GLA_V_W), tok)
    return pl.pallas_call(
        _gla_exact_body,
        grid=(batch, nt),
        in_specs=[kspec, kspec, vspec, kspec],
        out_specs=vspec,
        out_shape=jax.ShapeDtypeStruct((batch * seq, GLA_V_W), BF16),
        scratch_shapes=[pltpu.VMEM((GLA_HEADS, GLA_DK, GLA_DV), F32),
                        pltpu.VMEM((GLA_CHUNK, GLA_DK), F32), pltpu.VMEM((GLA_CHUNK, GLA_DK), F32)],
        compiler_params=_params(2),
        name="gla_exact",
    )(q, k, vb, bcum)


def _outproj_body(x_ref, oa_ref, ob_ref, rb_ref, ga_ref, gb_ref, og_ref, wa_ref, wb_ref, wo_ref, o_ref):
    quarter = D_MODEL // GLA_HEADS
    ya = []
    gated = []
    for h in range(GLA_HEADS):
        vs = slice(h * GLA_DV, (h + 1) * GLA_DV)
        ya.append(_dot(oa_ref[...], wa_ref[:, h * quarter:(h + 1) * quarter]))
        on = _rms(ob_ref[:, vs].astype(F32), og_ref[:, vs])
        gated.append((on * rb_ref[:, vs].astype(F32)).astype(BF16))
    gated = jnp.concatenate(gated, axis=1)
    merged = []
    for j in range(GLA_HEADS):
        cs = slice(j * quarter, (j + 1) * quarter)
        yb = _dot(gated, wb_ref[:, cs])
        merged.append((ga_ref[:, cs].astype(F32) * ya[j] + gb_ref[:, cs].astype(F32) * yb).astype(BF16))
    o_ref[...] = x_ref[...] + _dot(jnp.concatenate(merged, axis=1), wo_ref[...])


def _outproj(x, oa, ob, rb, ga, gb, out_gain, wa, wb, wo):
    n = x.shape[0]
    spec = lambda width: pl.BlockSpec((ROW_TILE, width), lambda i: (i, 0))
    (out,), _ = _dense_call(
        _outproj_body, "outproj", n,
        [spec(D_MODEL), spec(SWA_Q_W), spec(GLA_V_W), spec(GLA_V_W), spec(D_MODEL), spec(D_MODEL),
         _whole((1, GLA_V_W)), _whole((SWA_Q_W, D_MODEL)), _whole((GLA_V_W, D_MODEL)),
         _whole((D_MODEL, D_MODEL))],
        (x, oa, ob, rb, ga, gb, out_gain, wa, wb, wo),
        [spec(D_MODEL)], [jax.ShapeDtypeStruct((n, D_MODEL), F32)])
    return out


def _rope_tables(seq):
    inv_freq = ROPE_THETA ** (-jnp.arange(0, SWA_HEAD_DIM, 2, dtype=F32) / SWA_HEAD_DIM)
    ang = jnp.arange(seq, dtype=F32)[:, None] * inv_freq[None, :]
    cos, sin = jnp.cos(ang), jnp.sin(ang)
    reps = LANES // SWA_HEAD_DIM
    cos_t = jnp.tile(jnp.concatenate([cos, cos], axis=-1), (1, reps))
    sin_t = jnp.tile(jnp.concatenate([-sin, sin], axis=-1), (1, reps))
    return cos_t, sin_t


def kernel(x, ffn1_norm, ffn1_w_gate, ffn1_w_up, ffn1_w_down, mix_norm, w_in, swa_q_norm, swa_k_norm, swa_sinks, gla_w_gate, gla_gate_bias, gla_out_norm, w_proj_a, w_proj_b, w_out, ffn2_norm, ffn2_w_gate, ffn2_w_up, ffn2_w_down):
    batch, seq, d = x.shape
    assert d == D_MODEL and seq % ROW_TILE == 0
    n = batch * seq
    cos_t, sin_t = _rope_tables(seq)

    bf = lambda w: w.astype(BF16)
    wgate_p = jnp.concatenate(
        [gla_w_gate, jnp.zeros((DEPTH, GLR_PAD - GLA_GATE_RANK, GLA_K_W), F32)], axis=1).astype(BF16)
    head_id = jnp.arange(SWA_Q_W, dtype=jnp.int32) // SWA_HEAD_DIM
    head_ones = (head_id[:, None] == head_id[None, :]).astype(BF16)
    reps = LANES // SWA_HEAD_DIM
    q_gain = jnp.tile(swa_q_norm, (1, reps))
    k_gain = jnp.tile(swa_k_norm, (1, reps))
    row = lambda g, l: g[l:l + 1]

    w_in_t = jnp.transpose(w_in, (0, 2, 1))
    ffn1_w = (ffn1_w_gate, ffn1_w_up, ffn1_w_down)
    ffn2_w = (ffn2_w_gate, ffn2_w_up, ffn2_w_down)
    mix_w = (w_proj_a, w_proj_b, w_out)
    at = lambda ws, l: [(w, l) for w in ws]

    f1 = tuple(bf(w[0]) for w in ffn1_w)
    h = x.reshape(n, D_MODEL)
    for l in range(DEPTH):
        convs = [(w_in_t, l)] + at(ffn2_w, l) + (at(mix_w, 0) if l == 0 else [])
        h, conv = _ffn(h, row(ffn1_norm, l), *f1, convs=convs)
        w_t, f2 = conv[0], conv[1:4]
        if l == 0:
            wa, wb, wo = conv[4:]
        qa, ka, va, qe, qin, kin, kst, vb, rb, ga, gb, qraw, kraw, dec, bcum = _inproj(
            h, row(mix_norm, l), w_t, cos_t, sin_t, row(q_gain, l), row(k_gain, l), wgate_p[l],
            row(gla_gate_bias, l), head_ones, seq)
        oa = _swa(qa, ka, va, swa_sinks[l], batch, seq)
        ob = lax.cond(
            jnp.min(dec) >= GLA_MIN_CHUNK_DECAY,
            lambda *ops: _gla(*ops[:6], batch, seq),
            lambda *ops: _gla_exact(ops[6], ops[7], ops[4], ops[8], batch, seq),
            qe, qin, kin, kst, vb, dec, qraw, kraw, bcum)
        h = _outproj(h, oa, ob, rb, ga, gb, row(gla_out_norm, l), wa, wb, wo)
        convs = [] if l + 1 == DEPTH else at(ffn1_w, l + 1) + at(mix_w, l + 1)
        h, conv = _ffn(h, row(ffn2_norm, l), *f2, convs=convs)
        if convs:
            f1, (wa, wb, wo) = conv[:3], conv[3:]
    return h.reshape(batch, seq, D_MODEL)
```

```python
import jax
import jax.numpy as jnp
from jax import lax
from jax.experimental import pallas as pl
from jax.experimental.pallas import tpu as pltpu

F32 = jnp.float32
BF16 = jnp.bfloat16

D_MODEL = 1024
DEPTH = 4
D_FF = 2816
EPS = 1e-6

SWA_HEADS = 8
SWA_KV_HEADS = 2
SWA_GROUP = SWA_HEADS // SWA_KV_HEADS
SWA_HEAD_DIM = 64
SWA_BLOCK = 128
ROPE_THETA = 10000.0
SWA_Q_W = SWA_HEADS * SWA_HEAD_DIM
SWA_KV_W = SWA_KV_HEADS * SWA_HEAD_DIM

GLA_HEADS = 4
GLA_DK = 128
GLA_DV = 256
GLA_GATE_RANK = 16
GLA_TAU = 16.0
GLA_K_W = GLA_HEADS * GLA_DK
GLA_V_W = GLA_HEADS * GLA_DV

IN_SPLITS = (SWA_Q_W, SWA_KV_W, SWA_KV_W, GLA_K_W, GLA_K_W, GLA_V_W, GLA_V_W, GLA_GATE_RANK, D_MODEL, D_MODEL)

LANES = 128
GLR_PAD = LANES
IN_COLS = sum(IN_SPLITS)
IN_OFF = tuple(sum(IN_SPLITS[:j]) for j in range(len(IN_SPLITS)))
(P_QA, P_KA, P_VA, P_QB, P_KB, P_VB, P_RB, P_GLR, P_GA, P_GB) = range(len(IN_SPLITS))

ROW_TILE = 512
GLA_CHUNK = 128
CHUNKS_PER_TILE = ROW_TILE // GLA_CHUNK
GLA_FACTOR_ROWS = 3 * CHUNKS_PER_TILE
VMEM_LIMIT = 56 * 1024 * 1024
GLA_MIN_CHUNK_DECAY = 8.7e-27


def _rms(x, gain):
    ms = jnp.mean(x * x, axis=-1, keepdims=True)
    return x * lax.rsqrt(ms + EPS) * gain


def _sigmoid(x):
    return 0.5 * jnp.tanh(0.5 * x) + 0.5


def _dot(a, b):
    return jnp.dot(a, b, preferred_element_type=F32)


def _dot_nt(a, b):
    return lax.dot_general(a, b, (((1,), (1,)), ((), ())), preferred_element_type=F32)


def _dot_tn(a, b):
    return lax.dot_general(a, b, (((0,), (0,)), ((), ())), preferred_element_type=F32)


def _hi_lo(x):
    hi = x.astype(BF16)
    return hi, (x - hi.astype(F32)).astype(BF16)


def _resident(shape, index_map):
    return pl.BlockSpec(shape, index_map, pipeline_mode=pl.Buffered(1))


def _params(n_axes):
    return pltpu.CompilerParams(dimension_semantics=("arbitrary",) * n_axes,
                                vmem_limit_bytes=VMEM_LIMIT)


BF16_SUBLANES = 16


def _dense_call(body, name, n, in_specs, args, out_specs, out_shapes, convs=None):
    nsteps = n // ROW_TILE
    n_in, n_out, n_conv = len(in_specs), len(out_specs), len(convs or ())
    in_specs, out_specs, out_shapes, args = list(in_specs), list(out_specs), list(out_shapes), list(args)
    for w, layer in convs or ():
        _, rows, cols = w.shape
        nblk = max(b for b in range(1, nsteps + 1) if rows % (b * BF16_SUBLANES) == 0)
        last = nblk - 1
        in_specs.append(pl.BlockSpec((None, rows // nblk, cols),
                                     lambda i, layer=layer, last=last: (layer, jnp.minimum(i, last), 0)))
        out_specs.append(pl.BlockSpec((rows // nblk, cols), lambda i, last=last: (jnp.minimum(i, last), 0)))
        out_shapes.append(jax.ShapeDtypeStruct((rows, cols), BF16))
        args.append(w)

    def wrapped(*refs):
        ins, srcs = refs[:n_in], refs[n_in:n_in + n_conv]
        outs, dsts = refs[n_in + n_conv:n_in + n_conv + n_out], refs[n_in + n_conv + n_out:]

        def cast_weights():
            for s, d in zip(srcs, dsts):
                d[...] = s[...].astype(BF16)

        if convs is None:
            body(*ins, *outs)
        else:
            body(cast_weights, *ins, *outs)

    res = pl.pallas_call(
        wrapped, grid=(nsteps,), in_specs=in_specs, out_specs=out_specs, out_shape=out_shapes,
        compiler_params=_params(1), name=name)(*args)
    return res[:n_out], res[n_out:]


def _whole(shape):
    return _resident(shape, lambda i: (0,) * len(shape))


def _ffn_body(cast_weights, x_ref, gain_ref, wg_ref, wu_ref, wd_ref, o_ref):
    half = ROW_TILE // 2
    rows = (slice(0, half), slice(half, ROW_TILE))
    h = [None, None]
    gu = [None, None]
    h[0] = _rms(x_ref[rows[0], :], gain_ref[...]).astype(BF16)
    gu[0] = (_dot(h[0], wg_ref[...]), _dot(h[0], wu_ref[...]))
    h[1] = _rms(x_ref[rows[1], :], gain_ref[...]).astype(BF16)
    cast_weights()
    gu[1] = (_dot(h[1], wg_ref[...]), _dot(h[1], wu_ref[...]))
    for j in range(2):
        g, u = gu[j]
        a = (g * _sigmoid(g) * u).astype(BF16)
        o_ref[rows[j], :] = x_ref[rows[j], :] + 0.5 * _dot(a, wd_ref[...])


def _ffn(x, gain, wg, wu, wd, convs=()):
    n = x.shape[0]
    row = pl.BlockSpec((ROW_TILE, D_MODEL), lambda i: (i, 0))
    (out,), converted = _dense_call(
        _ffn_body, "ffn", n,
        [row, _whole((1, D_MODEL)), _whole((D_MODEL, D_FF)), _whole((D_MODEL, D_FF)), _whole((D_FF, D_MODEL))],
        (x, gain, wg, wu, wd), [row], [jax.ShapeDtypeStruct((n, D_MODEL), F32)], convs)
    return out, converted


def _head_norm_rope(x, ssq, gain, cos, sin_signed, lane):
    xn = x * lax.rsqrt(ssq * (1.0 / SWA_HEAD_DIM) + EPS) * gain
    half = SWA_HEAD_DIM // 2
    partner = jnp.where((lane % SWA_HEAD_DIM) < half,
                        pltpu.roll(xn, LANES - half, 1), pltpu.roll(xn, half, 1))
    return xn * cos + partner * sin_signed


def _inproj_body(x_ref, gain_ref, w_ref, cos_ref, sin_ref, qg_ref, kg_ref, wgate_ref, bias_ref,
                 ones_ref, qa_ref, ka_ref, va_ref, qin_ref, kin_ref, vb_ref, rb_ref,
                 ga_ref, gb_ref, qraw_ref, kraw_ref, fac_ref, bcum_ref):
    h = _rms(x_ref[...], gain_ref[...]).astype(BF16)
    half = D_MODEL // 2
    lane = lax.broadcasted_iota(jnp.int32, (1, LANES), 1)

    def proj(which, lo=0, width=None):
        off = IN_OFF[which] + lo
        width = IN_SPLITS[which] if width is None else width
        return _dot_nt(h, w_ref[off:off + width, :])

    narrow = proj(P_QA, 0, IN_OFF[P_QB])
    tail = proj(P_RB, half, half + LANES)
    glr = jnp.where(lane < GLA_GATE_RANK, tail[:, half:], 0.0).astype(BF16)
    logits = _dot(glr, wgate_ref[...]) + bias_ref[...]
    r = tail[:, :half]
    rb_ref[:, half:] = (r * _sigmoid(r)).astype(BF16)
    q = proj(P_QB) * GLA_DK ** -0.5
    la = (jnp.minimum(logits, 0.0) - jnp.log1p(jnp.exp(-jnp.abs(logits)))) * (1.0 / GLA_TAU)
    la_hi, la_lo = _hi_lo(la)
    k = proj(P_KB)
    qraw_ref[...] = q.astype(BF16)
    kraw_ref[...] = k.astype(BF16)

    c = GLA_CHUNK
    t_i = lax.broadcasted_iota(jnp.int32, (c, c), 0)
    s_i = lax.broadcasted_iota(jnp.int32, (c, c), 1)
    tril = (s_i <= t_i).astype(BF16)
    cums = []
    for ci in range(CHUNKS_PER_TILE):
        rows = slice(ci * c, (ci + 1) * c)
        cums.append(_dot(tril, la_hi[rows]) + _dot(tril, la_lo[rows]))

    def decay_products(ci):
        rows = slice(ci * c, (ci + 1) * c)
        b = cums[ci]
        b_last = b[c - 1:c, :]
        b_mid = b[c // 2 - 1:c // 2, :]
        qin_ref[rows, :] = (q[rows] * jnp.exp(b - b_mid)).astype(BF16)
        kin_ref[rows, :] = (k[rows] * jnp.exp(b_mid - b)).astype(BF16)
        fac_ref[0, ci:ci + 1, :] = jnp.exp(b_last)
        fac_ref[0, CHUNKS_PER_TILE + ci:CHUNKS_PER_TILE + ci + 1, :] = jnp.exp(b_mid)
        fac_ref[0, 2 * CHUNKS_PER_TILE + ci:2 * CHUNKS_PER_TILE + ci + 1, :] = jnp.exp(b_last - b_mid)
        bcum_ref[rows, :] = b

    cos, sin = cos_ref[...], sin_ref[...]
    qa = narrow[:, :SWA_Q_W]
    ssq_q = _dot((qa * qa).astype(BF16), ones_ref[...])

    def rope_q(col):
        cs = slice(col * LANES, (col + 1) * LANES)
        qc = _head_norm_rope(qa[:, cs], ssq_q[:, cs], qg_ref[...], cos, sin, lane)
        qa_ref[:, cs] = (qc * SWA_HEAD_DIM ** -0.5).astype(BF16)

    vb_ref[:, :half] = proj(P_VB, 0, half).astype(BF16)
    decay_products(0)
    vb_ref[:, half:] = proj(P_VB, half, half).astype(BF16)
    decay_products(1)
    r = proj(P_RB, 0, half)
    decay_products(2)
    rb_ref[:, :half] = (r * _sigmoid(r)).astype(BF16)
    decay_products(3)
    for col, (which, ref, lo) in enumerate(((P_GA, ga_ref, 0), (P_GA, ga_ref, half),
                                             (P_GB, gb_ref, 0), (P_GB, gb_ref, half))):
        ref[:, lo:lo + half] = _sigmoid(proj(which, lo, half)).astype(BF16)
        rope_q(col)
    ka = narrow[:, IN_OFF[P_KA]:IN_OFF[P_VA]]
    ssq_k = _dot((ka * ka).astype(BF16), ones_ref[0:SWA_KV_W, 0:SWA_KV_W])
    ka_ref[...] = _head_norm_rope(ka, ssq_k, kg_ref[...], cos, sin, lane).astype(BF16)
    va_ref[...] = narrow[:, IN_OFF[P_VA]:IN_OFF[P_QB]].astype(BF16)


def _inproj(x, gain, w, cos, sin, q_gain, k_gain, wgate, bias, head_ones, seq):
    n = x.shape[0]
    nt = n // ROW_TILE
    tiles_per_seq = seq // ROW_TILE
    row = lambda i: (i, 0)
    pos = lambda i: (i % tiles_per_seq, 0)
    widths = (SWA_Q_W, SWA_KV_W, SWA_KV_W, GLA_K_W, GLA_K_W,
              GLA_V_W, GLA_V_W, D_MODEL, D_MODEL, GLA_K_W, GLA_K_W)
    outs, _ = _dense_call(
        _inproj_body, "inproj", n,
        [pl.BlockSpec((ROW_TILE, D_MODEL), row), _whole((1, D_MODEL)), _whole((IN_COLS, D_MODEL)),
         pl.BlockSpec((ROW_TILE, LANES), pos), pl.BlockSpec((ROW_TILE, LANES), pos),
         _whole((1, LANES)), _whole((1, LANES)), _whole((GLR_PAD, GLA_K_W)), _whole((1, GLA_K_W)),
         _whole((SWA_Q_W, SWA_Q_W))],
        (x, gain, w, cos, sin, q_gain, k_gain, wgate, bias, head_ones),
        [pl.BlockSpec((ROW_TILE, width), row) for width in widths]
        + [pl.BlockSpec((1, GLA_FACTOR_ROWS, GLA_K_W), lambda i: (i, 0, 0)),
           pl.BlockSpec((ROW_TILE, GLA_K_W), row)],
        [jax.ShapeDtypeStruct((n, width), BF16) for width in widths]
        + [jax.ShapeDtypeStruct((nt, GLA_FACTOR_ROWS, GLA_K_W), F32),
           jax.ShapeDtypeStruct((n, GLA_K_W), F32)])
    return outs


def _swa_setup(i, k_ref, v_ref, kbuf, vbuf):
    lane = lax.broadcasted_iota(jnp.int32, (1, LANES), 1)
    first_half = lane < SWA_HEAD_DIM

    @pl.when(i == 0)
    def _():
        kbuf[0:SWA_BLOCK, :] = jnp.zeros((SWA_BLOCK, LANES), BF16)
        vbuf[0:SWA_BLOCK, :] = jnp.zeros((SWA_BLOCK, LANES), BF16)

    kbuf[SWA_BLOCK:, :] = k_ref[...]
    vbuf[SWA_BLOCK:, :] = v_ref[...]
    kall = kbuf[...]
    vall = vbuf[...]
    kswap = pltpu.roll(kall, SWA_HEAD_DIM, 1)
    vswap = pltpu.roll(vall, SWA_HEAD_DIM, 1)
    kdup = [jnp.where(first_half, kall, kswap), jnp.where(first_half, kswap, kall)]
    vdup = [jnp.where(first_half, vall, vswap), jnp.where(first_half, vswap, vall)]

    rows = SWA_GROUP * SWA_BLOCK
    t_idx = lax.broadcasted_iota(jnp.int32, (rows, SWA_BLOCK), 0) % SWA_BLOCK
    c_idx = lax.broadcasted_iota(jnp.int32, (rows, SWA_BLOCK), 1)
    use_cur = c_idx <= t_idx
    row_head = lax.broadcasted_iota(jnp.int32, (rows, 1), 0) // SWA_BLOCK
    return first_half, kdup, vdup, use_cur, row_head


def _swa_scores(ctx, q_ref, kvh, blk):
    first_half, kdup = ctx[0], ctx[1]
    r0 = blk * SWA_BLOCK
    parts = []
    for g in range(SWA_GROUP):
        head = kvh * SWA_GROUP + g
        qc = q_ref[r0:r0 + SWA_BLOCK, (head // 2) * LANES:(head // 2 + 1) * LANES]
        keep = first_half if head % 2 == 0 else jnp.logical_not(first_half)
        parts.append(jnp.where(keep, qc, jnp.zeros_like(qc)))
    return _dot_nt(jnp.concatenate(parts, axis=0), kdup[kvh][r0:r0 + 2 * SWA_BLOCK, :])


def _swa_attend(ctx, i, sinks_ref, o_ref, s2, kvh, blk):
    first_half, _, vdup, use_cur, row_head = ctx
    rows = SWA_GROUP * SWA_BLOCK
    sink = jnp.zeros((rows, 1), F32)
    for g in range(SWA_GROUP):
        sink = jnp.where(row_head == g, sinks_ref[kvh * SWA_GROUP + g], sink)
    r0 = blk * SWA_BLOCK
    s_prev = s2[:, :SWA_BLOCK]
    if blk == 0:
        s_prev = s_prev + jnp.where(i > 0, 0.0, -jnp.inf)
    s = jnp.where(use_cur, s2[:, SWA_BLOCK:], s_prev)
    m = jnp.maximum(jnp.max(s, axis=-1, keepdims=True), sink)
    p = jnp.exp(s - m)
    denom = jnp.sum(p, axis=-1, keepdims=True) + jnp.exp(sink - m)
    p2 = jnp.concatenate([jnp.where(use_cur, 0.0, p), jnp.where(use_cur, p, 0.0)], axis=1)
    vals = vdup[kvh][r0:r0 + 2 * SWA_BLOCK, :]
    o = _dot(p2.astype(BF16), vals) / denom
    for pair in range(SWA_GROUP // 2):
        a = o[(2 * pair) * SWA_BLOCK:(2 * pair + 1) * SWA_BLOCK, :]
        b = o[(2 * pair + 1) * SWA_BLOCK:(2 * pair + 2) * SWA_BLOCK, :]
        col = (kvh * SWA_GROUP) // 2 + pair
        o_ref[r0:r0 + SWA_BLOCK, col * LANES:(col + 1) * LANES] = (
            jnp.where(first_half, a, b).astype(o_ref.dtype))


def _swa_carry(kbuf, vbuf):
    kbuf[0:SWA_BLOCK, :] = kbuf[ROW_TILE:ROW_TILE + SWA_BLOCK, :]
    vbuf[0:SWA_BLOCK, :] = vbuf[ROW_TILE:ROW_TILE + SWA_BLOCK, :]


def _decay_columns(decay_row):
    cols = jnp.transpose(jnp.broadcast_to(decay_row, (GLA_CHUNK, GLA_DK)))
    return jnp.concatenate([cols] * (GLA_DV // GLA_CHUNK), axis=1)


def _gla_tile(i, qin_ref, kin_ref, v_ref, fac_ref, o_ref, s_ref):
    c = GLA_CHUNK

    @pl.when(i == 0)
    def _():
        s_ref[...] = jnp.zeros(s_ref.shape, F32)

    t_i = lax.broadcasted_iota(jnp.int32, (c, c), 0)
    s_i = lax.broadcasted_iota(jnp.int32, (c, c), 1)
    causal = s_i <= t_i
    items = [(ci, h) for ci in range(CHUNKS_PER_TILE) for h in range(GLA_HEADS)]
    window = lambda ci, h: (slice(ci * c, (ci + 1) * c), slice(h * GLA_DK, (h + 1) * GLA_DK),
                            slice(h * GLA_DV, (h + 1) * GLA_DV))
    factor = lambda kind, ci, ks: fac_ref[0, kind * CHUNKS_PER_TILE + ci:kind * CHUNKS_PER_TILE + ci + 1, ks]

    attn, update = {}, {}
    for ci, h in items:
        rows, ks, vs = window(ci, h)
        scores = _dot_nt(qin_ref[rows, ks], kin_ref[rows, ks])
        attn[ci, h] = jnp.where(causal, scores, 0.0).astype(BF16)
        k_state = (kin_ref[rows, ks].astype(F32) * factor(2, ci, ks)).astype(BF16)
        update[ci, h] = _dot_tn(k_state, v_ref[rows, vs])

    states = [s_ref[h] for h in range(GLA_HEADS)]
    for ci, h in items:
        rows, ks, vs = window(ci, h)
        q_decayed = (qin_ref[rows, ks].astype(F32) * factor(1, ci, ks)).astype(BF16)
        lhs = jnp.concatenate([q_decayed, attn[ci, h]], axis=1)
        rhs = jnp.concatenate([states[h].astype(BF16), v_ref[rows, vs]], axis=0)
        o_ref[rows, vs] = _dot(lhs, rhs).astype(o_ref.dtype)
        states[h] = _decay_columns(factor(0, ci, ks)) * states[h] + update[ci, h]
    for h in range(GLA_HEADS):
        s_ref[h] = states[h]


def _swa_tile(i, sinks_ref, q_ref, k_ref, v_ref, o_ref, kbuf, vbuf):
    ctx = _swa_setup(i, k_ref, v_ref, kbuf, vbuf)
    items = [(kvh, blk) for kvh in range(SWA_KV_HEADS) for blk in range(ROW_TILE // SWA_BLOCK)]
    scores = [_swa_scores(ctx, q_ref, kvh, blk) for kvh, blk in items]
    for s2, (kvh, blk) in zip(scores, items):
        _swa_attend(ctx, i, sinks_ref, o_ref, s2, kvh, blk)
    _swa_carry(kbuf, vbuf)


def _core_body(sinks_ref, qa_ref, ka_ref, va_ref, qin_ref, kin_ref, vb_ref, fac_ref,
               oa_ref, ob_ref, kbuf, vbuf, s_ref):
    i = pl.program_id(1)
    _gla_tile(i, qin_ref, kin_ref, vb_ref, fac_ref, ob_ref, s_ref)
    _swa_tile(i, sinks_ref, qa_ref, ka_ref, va_ref, oa_ref, kbuf, vbuf)


def _swa_scratch():
    return [pltpu.VMEM((ROW_TILE + SWA_BLOCK, LANES), BF16), pltpu.VMEM((ROW_TILE + SWA_BLOCK, LANES), BF16)]


def _core(qa, ka, va, sinks, qin, kin, vb, fac, batch, seq):
    nt = seq // ROW_TILE
    tok = lambda b, i, *_: (b * nt + i, 0)
    spec = lambda width: pl.BlockSpec((ROW_TILE, width), tok)
    return pl.pallas_call(
        _core_body,
        grid_spec=pltpu.PrefetchScalarGridSpec(
            num_scalar_prefetch=1,
            grid=(batch, nt),
            in_specs=[spec(SWA_Q_W), spec(SWA_KV_W), spec(SWA_KV_W),
                      spec(GLA_K_W), spec(GLA_K_W), spec(GLA_V_W),
                      pl.BlockSpec((1, GLA_FACTOR_ROWS, GLA_K_W), lambda b, i, *_: (b * nt + i, 0, 0))],
            out_specs=[spec(SWA_Q_W), spec(GLA_V_W)],
            scratch_shapes=_swa_scratch() + [pltpu.VMEM((GLA_HEADS, GLA_DK, GLA_DV), F32)],
        ),
        out_shape=[jax.ShapeDtypeStruct((batch * seq, SWA_Q_W), BF16),
                   jax.ShapeDtypeStruct((batch * seq, GLA_V_W), BF16)],
        compiler_params=_params(2),
        name="core",
    )(sinks, qa, ka, va, qin, kin, vb, fac)


def _swa_body(sinks_ref, q_ref, k_ref, v_ref, o_ref, kbuf, vbuf):
    _swa_tile(pl.program_id(1), sinks_ref, q_ref, k_ref, v_ref, o_ref, kbuf, vbuf)


def _swa(qa, ka, va, sinks, batch, seq):
    nt = seq // ROW_TILE
    tok = lambda b, i, *_: (b * nt + i, 0)
    return pl.pallas_call(
        _swa_body,
        grid_spec=pltpu.PrefetchScalarGridSpec(
            num_scalar_prefetch=1,
            grid=(batch, nt),
            in_specs=[
                pl.BlockSpec((ROW_TILE, SWA_Q_W), tok),
                pl.BlockSpec((ROW_TILE, SWA_KV_W), tok),
                pl.BlockSpec((ROW_TILE, SWA_KV_W), tok),
            ],
            out_specs=pl.BlockSpec((ROW_TILE, SWA_Q_W), tok),
            scratch_shapes=_swa_scratch(),
        ),
        out_shape=jax.ShapeDtypeStruct((batch * seq, SWA_Q_W), BF16),
        compiler_params=_params(2),
        name="swa",
    )(sinks, qa, ka, va)


def _gla_exact_body(q_ref, k_ref, v_ref, b_ref, o_ref, s_ref, kf_ref, bf_ref):
    i = pl.program_id(1)
    c = GLA_CHUNK

    @pl.when(i == 0)
    def _():
        s_ref[...] = jnp.zeros(s_ref.shape, F32)

    t_col = lax.broadcasted_iota(jnp.int32, (c, 1), 0)
    s_lane = lax.broadcasted_iota(jnp.int32, (1, c), 1)

    for h in range(GLA_HEADS):
        ks = slice(h * GLA_DK, (h + 1) * GLA_DK)
        vs = slice(h * GLA_DV, (h + 1) * GLA_DV)

        def chunk_step(ci, carry, h=h, ks=ks, vs=vs):
            rows = pl.ds(pl.multiple_of(ci * c, c), c)
            b = b_ref[rows, ks]
            q = q_ref[rows, ks].astype(F32)
            k = k_ref[rows, ks].astype(F32)
            v = v_ref[rows, vs]
            kf_ref[...] = k
            bf_ref[...] = b

            def key_column(s, attn):
                pair = jnp.exp(jnp.minimum(b - bf_ref[pl.ds(s, 1), :], 0.0))
                col = jnp.sum(q * pair * kf_ref[pl.ds(s, 1), :], axis=-1, keepdims=True)
                col = jnp.where(t_col >= s, col, 0.0)
                return attn + col * (s_lane == s).astype(F32)

            attn = lax.fori_loop(0, c, key_column, jnp.zeros((c, c), F32))
            b_last = b[c - 1:c, :]
            state = s_ref[h]
            o = _dot((q * jnp.exp(b)).astype(BF16), state.astype(BF16)) + _dot(attn.astype(BF16), v)
            o_ref[rows, vs] = o.astype(o_ref.dtype)
            s_ref[h] = (_decay_columns(jnp.exp(b_last)) * state
                        + _dot_tn((k * jnp.exp(b_last - b)).astype(BF16), v))
            return carry

        lax.fori_loop(0, CHUNKS_PER_TILE, chunk_step, 0)


def _gla_exact(q, k, vb, bcum, batch, seq):
    nt = seq // ROW_TILE
    tok = lambda b, i: (b * nt + i, 0)
    kspec = pl.BlockSpec((ROW_TILE, GLA_K_W), tok)
    vspec = pl.BlockSpec((ROW_TILE, GLA_V_W), tok)
    return pl.pallas_call(
        _gla_exact_body,
        grid=(batch, nt),
        in_specs=[kspec, kspec, vspec, kspec],
        out_specs=vspec,
        out_shape=jax.ShapeDtypeStruct((batch * seq, GLA_V_W), BF16),
        scratch_shapes=[pltpu.VMEM((GLA_HEADS, GLA_DK, GLA_DV), F32),
                        pltpu.VMEM((GLA_CHUNK, GLA_DK), F32), pltpu.VMEM((GLA_CHUNK, GLA_DK), F32)],
        compiler_params=_params(2),
        name="gla_exact",
    )(q, k, vb, bcum)


def _outproj_body(x_ref, oa_ref, ob_ref, rb_ref, ga_ref, gb_ref, og_ref, wa_ref, wb_ref, wo_ref, o_ref):
    quarter = D_MODEL // GLA_HEADS
    ya = []
    gated = []
    for h in range(GLA_HEADS):
        vs = slice(h * GLA_DV, (h + 1) * GLA_DV)
        ya.append(_dot(oa_ref[...], wa_ref[:, h * quarter:(h + 1) * quarter]))
        on = _rms(ob_ref[:, vs].astype(F32), og_ref[:, vs])
        gated.append((on * rb_ref[:, vs].astype(F32)).astype(BF16))
    gated = jnp.concatenate(gated, axis=1)
    merged = []
    for j in range(GLA_HEADS):
        cs = slice(j * quarter, (j + 1) * quarter)
        yb = _dot(gated, wb_ref[:, cs])
        merged.append((ga_ref[:, cs].astype(F32) * ya[j] + gb_ref[:, cs].astype(F32) * yb).astype(BF16))
    o_ref[...] = x_ref[...] + _dot(jnp.concatenate(merged, axis=1), wo_ref[...])


def _outproj(x, oa, ob, rb, ga, gb, out_gain, wa, wb, wo):
    n = x.shape[0]
    spec = lambda width: pl.BlockSpec((ROW_TILE, width), lambda i: (i, 0))
    (out,), _ = _dense_call(
        _outproj_body, "outproj", n,
        [spec(D_MODEL), spec(SWA_Q_W), spec(GLA_V_W), spec(GLA_V_W), spec(D_MODEL), spec(D_MODEL),
         _whole((1, GLA_V_W)), _whole((SWA_Q_W, D_MODEL)), _whole((GLA_V_W, D_MODEL)),
         _whole((D_MODEL, D_MODEL))],
        (x, oa, ob, rb, ga, gb, out_gain, wa, wb, wo),
        [spec(D_MODEL)], [jax.ShapeDtypeStruct((n, D_MODEL), F32)])
    return out


def _rope_tables(seq):
    inv_freq = ROPE_THETA ** (-jnp.arange(0, SWA_HEAD_DIM, 2, dtype=F32) / SWA_HEAD_DIM)
    ang = jnp.arange(seq, dtype=F32)[:, None] * inv_freq[None, :]
    cos, sin = jnp.cos(ang), jnp.sin(ang)
    reps = LANES // SWA_HEAD_DIM
    cos_t = jnp.tile(jnp.concatenate([cos, cos], axis=-1), (1, reps))
    sin_t = jnp.tile(jnp.concatenate([-sin, sin], axis=-1), (1, reps))
    return cos_t, sin_t


def kernel(x, ffn1_norm, ffn1_w_gate, ffn1_w_up, ffn1_w_down, mix_norm, w_in, swa_q_norm, swa_k_norm, swa_sinks, gla_w_gate, gla_gate_bias, gla_out_norm, w_proj_a, w_proj_b, w_out, ffn2_norm, ffn2_w_gate, ffn2_w_up, ffn2_w_down):
    batch, seq, d = x.shape
    assert d == D_MODEL and seq % ROW_TILE == 0
    n = batch * seq
    cos_t, sin_t = _rope_tables(seq)

    bf = lambda w: w.astype(BF16)
    wgate_p = jnp.concatenate(
        [gla_w_gate, jnp.zeros((DEPTH, GLR_PAD - GLA_GATE_RANK, GLA_K_W), F32)], axis=1).astype(BF16)
    head_id = jnp.arange(SWA_Q_W, dtype=jnp.int32) // SWA_HEAD_DIM
    head_ones = (head_id[:, None] == head_id[None, :]).astype(BF16)
    reps = LANES // SWA_HEAD_DIM
    q_gain = jnp.tile(swa_q_norm, (1, reps))
    k_gain = jnp.tile(swa_k_norm, (1, reps))
    row = lambda g, l: g[l:l + 1]

    w_in_t = jnp.transpose(w_in, (0, 2, 1))
    ffn1_w = (ffn1_w_gate, ffn1_w_up, ffn1_w_down)
    ffn2_w = (ffn2_w_gate, ffn2_w_up, ffn2_w_down)
    mix_w = (w_proj_a, w_proj_b, w_out)
    at = lambda ws, l: [(w, l) for w in ws]

    f1 = tuple(bf(w[0]) for w in ffn1_w)
    h = x.reshape(n, D_MODEL)
    for l in range(DEPTH):
        convs = [(w_in_t, l)] + at(ffn2_w, l) + (at(mix_w, 0) if l == 0 else [])
        h, conv = _ffn(h, row(ffn1_norm, l), *f1, convs=convs)
        w_t, f2 = conv[0], conv[1:4]
        if l == 0:
            wa, wb, wo = conv[4:]
        qa, ka, va, qin, kin, vb, rb, ga, gb, qraw, kraw, fac, bcum = _inproj(
            h, row(mix_norm, l), w_t, cos_t, sin_t, row(q_gain, l), row(k_gain, l), wgate_p[l],
            row(gla_gate_bias, l), head_ones, seq)
        oa, ob = lax.cond(
            jnp.min(fac[:, :CHUNKS_PER_TILE]) >= GLA_MIN_CHUNK_DECAY,
            lambda *ops: tuple(_core(*ops[:8], batch, seq)),
            lambda *ops: (_swa(*ops[:4], batch, seq),
                          _gla_exact(ops[8], ops[9], ops[6], ops[10], batch, seq)),
            qa, ka, va, swa_sinks[l], qin, kin, vb, fac, qraw, kraw, bcum)
        h = _outproj(h, oa, ob, rb, ga, gb, row(gla_out_norm, l), wa, wb, wo)
        convs = [] if l + 1 == DEPTH else at(ffn1_w, l + 1) + at(mix_w, l + 1)
        h, conv = _ffn(h, row(ffn2_norm, l), *f2, convs=convs)
        if convs:
            f1, (wa, wb, wo) = conv[:3], conv[3:]
    return h.reshape(batch, seq, D_MODEL)
```

```python
import jax
import jax.numpy as jnp
from jax import lax
from jax.experimental import pallas as pl
from jax.experimental.pallas import tpu as pltpu

F32 = jnp.float32
BF16 = jnp.bfloat16

D_MODEL = 1024
DEPTH = 4
D_FF = 2816
EPS = 1e-6

SWA_HEADS = 8
SWA_KV_HEADS = 2
SWA_GROUP = SWA_HEADS // SWA_KV_HEADS
SWA_HEAD_DIM = 64
SWA_BLOCK = 128
ROPE_THETA = 10000.0
SWA_Q_W = SWA_HEADS * SWA_HEAD_DIM
SWA_KV_W = SWA_KV_HEADS * SWA_HEAD_DIM

GLA_HEADS = 4
GLA_DK = 128
GLA_DV = 256
GLA_GATE_RANK = 16
GLA_TAU = 16.0
GLA_K_W = GLA_HEADS * GLA_DK
GLA_V_W = GLA_HEADS * GLA_DV

IN_SPLITS = (SWA_Q_W, SWA_KV_W, SWA_KV_W, GLA_K_W, GLA_K_W, GLA_V_W, GLA_V_W, GLA_GATE_RANK, D_MODEL, D_MODEL)

LANES = 128
GLR_PAD = LANES
IN_COLS = sum(IN_SPLITS)
IN_OFF = tuple(sum(IN_SPLITS[:j]) for j in range(len(IN_SPLITS)))
(P_QA, P_KA, P_VA, P_QB, P_KB, P_VB, P_RB, P_GLR, P_GA, P_GB) = range(len(IN_SPLITS))

ROW_TILE = 512
GLA_CHUNK = 128
CHUNKS_PER_TILE = ROW_TILE // GLA_CHUNK
GLA_FACTOR_ROWS = 3 * CHUNKS_PER_TILE
VMEM_LIMIT = 56 * 1024 * 1024
GLA_MIN_CHUNK_DECAY = 8.7e-27


def _rms(x, gain):
    ms = jnp.mean(x * x, axis=-1, keepdims=True)
    return x * lax.rsqrt(ms + EPS) * gain


def _sigmoid(x):
    return 0.5 * jnp.tanh(0.5 * x) + 0.5


def _dot(a, b):
    return jnp.dot(a, b, preferred_element_type=F32)


def _dot_nt(a, b):
    return lax.dot_general(a, b, (((1,), (1,)), ((), ())), preferred_element_type=F32)


def _dot_tn(a, b):
    return lax.dot_general(a, b, (((0,), (0,)), ((), ())), preferred_element_type=F32)


def _hi_lo(x):
    hi = x.astype(BF16)
    return hi, (x - hi.astype(F32)).astype(BF16)


def _resident(shape, index_map):
    return pl.BlockSpec(shape, index_map, pipeline_mode=pl.Buffered(1))


def _params(n_axes):
    return pltpu.CompilerParams(dimension_semantics=("arbitrary",) * n_axes,
                                vmem_limit_bytes=VMEM_LIMIT)


BF16_SUBLANES = 16


def _dense_call(body, name, n, in_specs, args, out_specs, out_shapes, convs=None):
    nsteps = n // ROW_TILE
    n_in, n_out, n_conv = len(in_specs), len(out_specs), len(convs or ())
    in_specs, out_specs, out_shapes, args = list(in_specs), list(out_specs), list(out_shapes), list(args)
    for w, layer in convs or ():
        _, rows, cols = w.shape
        nblk = max(b for b in range(1, nsteps + 1) if rows % (b * BF16_SUBLANES) == 0)
        last = nblk - 1
        in_specs.append(pl.BlockSpec((None, rows // nblk, cols),
                                     lambda i, layer=layer, last=last: (layer, jnp.minimum(i, last), 0)))
        out_specs.append(pl.BlockSpec((rows // nblk, cols), lambda i, last=last: (jnp.minimum(i, last), 0)))
        out_shapes.append(jax.ShapeDtypeStruct((rows, cols), BF16))
        args.append(w)

    def wrapped(*refs):
        ins, srcs = refs[:n_in], refs[n_in:n_in + n_conv]
        outs, dsts = refs[n_in + n_conv:n_in + n_conv + n_out], refs[n_in + n_conv + n_out:]

        def cast_weights():
            for s, d in zip(srcs, dsts):
                d[...] = s[...].astype(BF16)

        if convs is None:
            body(*ins, *outs)
        else:
            body(cast_weights, *ins, *outs)

    res = pl.pallas_call(
        wrapped, grid=(nsteps,), in_specs=in_specs, out_specs=out_specs, out_shape=out_shapes,
        compiler_params=_params(1), name=name)(*args)
    return res[:n_out], res[n_out:]


def _whole(shape):
    return _resident(shape, lambda i: (0,) * len(shape))


def _ffn_body(cast_weights, x_ref, gain_ref, wg_ref, wu_ref, wd_ref, o_ref):
    half = ROW_TILE // 2
    rows = (slice(0, half), slice(half, ROW_TILE))
    h = [None, None]
    gu = [None, None]
    h[0] = _rms(x_ref[rows[0], :], gain_ref[...]).astype(BF16)
    gu[0] = (_dot(h[0], wg_ref[...]), _dot(h[0], wu_ref[...]))
    h[1] = _rms(x_ref[rows[1], :], gain_ref[...]).astype(BF16)
    cast_weights()
    gu[1] = (_dot(h[1], wg_ref[...]), _dot(h[1], wu_ref[...]))
    for j in range(2):
        g, u = gu[j]
        a = (g * _sigmoid(g) * u).astype(BF16)
        o_ref[rows[j], :] = x_ref[rows[j], :] + 0.5 * _dot(a, wd_ref[...])


def _ffn(x, gain, wg, wu, wd, convs=()):
    n = x.shape[0]
    row = pl.BlockSpec((ROW_TILE, D_MODEL), lambda i: (i, 0))
    (out,), converted = _dense_call(
        _ffn_body, "ffn", n,
        [row, _whole((1, D_MODEL)), _whole((D_MODEL, D_FF)), _whole((D_MODEL, D_FF)), _whole((D_FF, D_MODEL))],
        (x, gain, wg, wu, wd), [row], [jax.ShapeDtypeStruct((n, D_MODEL), F32)], convs)
    return out, converted


def _head_norm_rope(x, ssq, gain, cos, sin_signed, lane):
    xn = x * lax.rsqrt(ssq * (1.0 / SWA_HEAD_DIM) + EPS) * gain
    half = SWA_HEAD_DIM // 2
    partner = jnp.where((lane % SWA_HEAD_DIM) < half,
                        pltpu.roll(xn, LANES - half, 1), pltpu.roll(xn, half, 1))
    return xn * cos + partner * sin_signed


def _inproj_body(x_ref, gain_ref, w_ref, cos_ref, sin_ref, qg_ref, kg_ref, wgate_ref, bias_ref,
                 ones_ref, qa_ref, ka_ref, va_ref, qin_ref, kin_ref, vb_ref, rb_ref,
                 ga_ref, gb_ref, qraw_ref, kraw_ref, fac_ref, bcum_ref):
    h = _rms(x_ref[...], gain_ref[...]).astype(BF16)
    half = D_MODEL // 2
    lane = lax.broadcasted_iota(jnp.int32, (1, LANES), 1)

    def proj(which, lo=0, width=None):
        off = IN_OFF[which] + lo
        width = IN_SPLITS[which] if width is None else width
        return _dot_nt(h, w_ref[off:off + width, :])

    narrow = proj(P_QA, 0, IN_OFF[P_QB])
    tail = proj(P_RB, half, half + LANES)
    glr = jnp.where(lane < GLA_GATE_RANK, tail[:, half:], 0.0).astype(BF16)
    logits = _dot(glr, wgate_ref[...]) + bias_ref[...]
    r = tail[:, :half]
    rb_ref[:, half:] = (r * _sigmoid(r)).astype(BF16)
    q = proj(P_QB) * GLA_DK ** -0.5
    la = (jnp.minimum(logits, 0.0) - jnp.log1p(jnp.exp(-jnp.abs(logits)))) * (1.0 / GLA_TAU)
    la_hi, la_lo = _hi_lo(la)
    k = proj(P_KB)
    qraw_ref[...] = q.astype(BF16)
    kraw_ref[...] = k.astype(BF16)

    c = GLA_CHUNK
    t_i = lax.broadcasted_iota(jnp.int32, (c, c), 0)
    s_i = lax.broadcasted_iota(jnp.int32, (c, c), 1)
    tril = (s_i <= t_i).astype(BF16)
    cums = []
    for ci in range(CHUNKS_PER_TILE):
        rows = slice(ci * c, (ci + 1) * c)
        cums.append(_dot(tril, la_hi[rows]) + _dot(tril, la_lo[rows]))

    def decay_products(ci):
        rows = slice(ci * c, (ci + 1) * c)
        b = cums[ci]
        b_last = b[c - 1:c, :]
        b_mid = b[c // 2 - 1:c // 2, :]
        qin_ref[rows, :] = (q[rows] * jnp.exp(b - b_mid)).astype(BF16)
        kin_ref[rows, :] = (k[rows] * jnp.exp(b_mid - b)).astype(BF16)
        fac_ref[0, ci:ci + 1, :] = jnp.exp(b_last)
        fac_ref[0, CHUNKS_PER_TILE + ci:CHUNKS_PER_TILE + ci + 1, :] = jnp.exp(b_mid)
        fac_ref[0, 2 * CHUNKS_PER_TILE + ci:2 * CHUNKS_PER_TILE + ci + 1, :] = jnp.exp(b_last - b_mid)
        bcum_ref[rows, :] = b

    cos, sin = cos_ref[...], sin_ref[...]
    qa = narrow[:, :SWA_Q_W]
    ssq_q = _dot((qa * qa).astype(BF16), ones_ref[...])

    def rope_q(col):
        cs = slice(col * LANES, (col + 1) * LANES)
        qc = _head_norm_rope(qa[:, cs], ssq_q[:, cs], qg_ref[...], cos, sin, lane)
        qa_ref[:, cs] = (qc * SWA_HEAD_DIM ** -0.5).astype(BF16)

    vb_ref[:, :half] = proj(P_VB, 0, half).astype(BF16)
    decay_products(0)
    vb_ref[:, half:] = proj(P_VB, half, half).astype(BF16)
    decay_products(1)
    r = proj(P_RB, 0, half)
    decay_products(2)
    rb_ref[:, :half] = (r * _sigmoid(r)).astype(BF16)
    decay_products(3)
    for col, (which, ref, lo) in enumerate(((P_GA, ga_ref, 0), (P_GA, ga_ref, half),
                                             (P_GB, gb_ref, 0), (P_GB, gb_ref, half))):
        ref[:, lo:lo + half] = _sigmoid(proj(which, lo, half)).astype(BF16)
        rope_q(col)
    ka = narrow[:, IN_OFF[P_KA]:IN_OFF[P_VA]]
    ssq_k = _dot((ka * ka).astype(BF16), ones_ref[0:SWA_KV_W, 0:SWA_KV_W])
    ka_ref[...] = _head_norm_rope(ka, ssq_k, kg_ref[...], cos, sin, lane).astype(BF16)
    va_ref[...] = narrow[:, IN_OFF[P_VA]:IN_OFF[P_QB]].astype(BF16)


def _inproj(x, gain, w, cos, sin, q_gain, k_gain, wgate, bias, head_ones, seq):
    n = x.shape[0]
    nt = n // ROW_TILE
    tiles_per_seq = seq // ROW_TILE
    row = lambda i: (i, 0)
    pos = lambda i: (i % tiles_per_seq, 0)
    widths = (SWA_Q_W, SWA_KV_W, SWA_KV_W, GLA_K_W, GLA_K_W,
              GLA_V_W, GLA_V_W, D_MODEL, D_MODEL, GLA_K_W, GLA_K_W)
    outs, _ = _dense_call(
        _inproj_body, "inproj", n,
        [pl.BlockSpec((ROW_TILE, D_MODEL), row), _whole((1, D_MODEL)), _whole((IN_COLS, D_MODEL)),
         pl.BlockSpec((ROW_TILE, LANES), pos), pl.BlockSpec((ROW_TILE, LANES), pos),
         _whole((1, LANES)), _whole((1, LANES)), _whole((GLR_PAD, GLA_K_W)), _whole((1, GLA_K_W)),
         _whole((SWA_Q_W, SWA_Q_W))],
        (x, gain, w, cos, sin, q_gain, k_gain, wgate, bias, head_ones),
        [pl.BlockSpec((ROW_TILE, width), row) for width in widths]
        + [pl.BlockSpec((1, GLA_FACTOR_ROWS, GLA_K_W), lambda i: (i, 0, 0)),
           pl.BlockSpec((ROW_TILE, GLA_K_W), row)],
        [jax.ShapeDtypeStruct((n, width), BF16) for width in widths]
        + [jax.ShapeDtypeStruct((nt, GLA_FACTOR_ROWS, GLA_K_W), F32),
           jax.ShapeDtypeStruct((n, GLA_K_W), F32)])
    return outs


def _swa_setup(i, k_ref, v_ref, kbuf, vbuf):
    lane = lax.broadcasted_iota(jnp.int32, (1, LANES), 1)
    first_half = lane < SWA_HEAD_DIM

    @pl.when(i == 0)
    def _():
        kbuf[0:SWA_BLOCK, :] = jnp.zeros((SWA_BLOCK, LANES), BF16)
        vbuf[0:SWA_BLOCK, :] = jnp.zeros((SWA_BLOCK, LANES), BF16)

    kbuf[SWA_BLOCK:, :] = k_ref[...]
    vbuf[SWA_BLOCK:, :] = v_ref[...]
    kall = kbuf[...]
    vall = vbuf[...]
    kswap = pltpu.roll(kall, SWA_HEAD_DIM, 1)
    vswap = pltpu.roll(vall, SWA_HEAD_DIM, 1)
    kdup = [jnp.where(first_half, kall, kswap), jnp.where(first_half, kswap, kall)]
    vdup = [jnp.where(first_half, vall, vswap), jnp.where(first_half, vswap, vall)]

    rows = SWA_GROUP * SWA_BLOCK
    t_idx = lax.broadcasted_iota(jnp.int32, (rows, SWA_BLOCK), 0) % SWA_BLOCK
    c_idx = lax.broadcasted_iota(jnp.int32, (rows, SWA_BLOCK), 1)
    use_cur = c_idx <= t_idx
    row_head = lax.broadcasted_iota(jnp.int32, (rows, 1), 0) // SWA_BLOCK
    return first_half, kdup, vdup, use_cur, row_head


def _swa_scores(ctx, q_ref, kvh, blk):
    first_half, kdup = ctx[0], ctx[1]
    r0 = blk * SWA_BLOCK
    parts = []
    for g in range(SWA_GROUP):
        head = kvh * SWA_GROUP + g
        qc = q_ref[r0:r0 + SWA_BLOCK, (head // 2) * LANES:(head // 2 + 1) * LANES]
        keep = first_half if head % 2 == 0 else jnp.logical_not(first_half)
        parts.append(jnp.where(keep, qc, jnp.zeros_like(qc)))
    return _dot_nt(jnp.concatenate(parts, axis=0), kdup[kvh][r0:r0 + 2 * SWA_BLOCK, :])


def _swa_attend(ctx, i, sinks_ref, o_ref, s2, kvh, blk):
    first_half, _, vdup, use_cur, row_head = ctx
    rows = SWA_GROUP * SWA_BLOCK
    sink = jnp.zeros((rows, 1), F32)
    for g in range(SWA_GROUP):
        sink = jnp.where(row_head == g, sinks_ref[kvh * SWA_GROUP + g], sink)
    r0 = blk * SWA_BLOCK
    s_prev = s2[:, :SWA_BLOCK]
    if blk == 0:
        s_prev = s_prev + jnp.where(i > 0, 0.0, -jnp.inf)
    s = jnp.where(use_cur, s2[:, SWA_BLOCK:], s_prev)
    m = jnp.maximum(jnp.max(s, axis=-1, keepdims=True), sink)
    p = jnp.exp(s - m)
    denom = jnp.sum(p, axis=-1, keepdims=True) + jnp.exp(sink - m)
    p2 = jnp.concatenate([jnp.where(use_cur, 0.0, p), jnp.where(use_cur, p, 0.0)], axis=1)
    vals = vdup[kvh][r0:r0 + 2 * SWA_BLOCK, :]
    o = _dot(p2.astype(BF16), vals) / denom
    for pair in range(SWA_GROUP // 2):
        a = o[(2 * pair) * SWA_BLOCK:(2 * pair + 1) * SWA_BLOCK, :]
        b = o[(2 * pair + 1) * SWA_BLOCK:(2 * pair + 2) * SWA_BLOCK, :]
        col = (kvh * SWA_GROUP) // 2 + pair
        o_ref[r0:r0 + SWA_BLOCK, col * LANES:(col + 1) * LANES] = (
            jnp.where(first_half, a, b).astype(o_ref.dtype))


def _swa_carry(kbuf, vbuf):
    kbuf[0:SWA_BLOCK, :] = kbuf[ROW_TILE:ROW_TILE + SWA_BLOCK, :]
    vbuf[0:SWA_BLOCK, :] = vbuf[ROW_TILE:ROW_TILE + SWA_BLOCK, :]


def _decay_columns(decay_row):
    cols = jnp.transpose(jnp.broadcast_to(decay_row, (GLA_CHUNK, GLA_DK)))
    return jnp.concatenate([cols] * (GLA_DV // GLA_CHUNK), axis=1)


def _gla_tile(i, qin_ref, kin_ref, v_ref, fac_ref, o_ref, s_ref):
    c = GLA_CHUNK

    @pl.when(i == 0)
    def _():
        s_ref[...] = jnp.zeros(s_ref.shape, F32)

    t_i = lax.broadcasted_iota(jnp.int32, (c, c), 0)
    s_i = lax.broadcasted_iota(jnp.int32, (c, c), 1)
    causal = s_i <= t_i
    items = [(ci, h) for ci in range(CHUNKS_PER_TILE) for h in range(GLA_HEADS)]
    window = lambda ci, h: (slice(ci * c, (ci + 1) * c), slice(h * GLA_DK, (h + 1) * GLA_DK),
                            slice(h * GLA_DV, (h + 1) * GLA_DV))
    factor = lambda kind, ci, ks: fac_ref[0, kind * CHUNKS_PER_TILE + ci:kind * CHUNKS_PER_TILE + ci + 1, ks]

    attn, update = {}, {}
    for ci, h in items:
        rows, ks, vs = window(ci, h)
        scores = _dot_nt(qin_ref[rows, ks], kin_ref[rows, ks])
        attn[ci, h] = jnp.where(causal, scores, 0.0).astype(BF16)
        k_state = (kin_ref[rows, ks].astype(F32) * factor(2, ci, ks)).astype(BF16)
        update[ci, h] = _dot_tn(k_state, v_ref[rows, vs])

    states = [s_ref[h] for h in range(GLA_HEADS)]
    for ci, h in items:
        rows, ks, vs = window(ci, h)
        q_decayed = (qin_ref[rows, ks].astype(F32) * factor(1, ci, ks)).astype(BF16)
        lhs = jnp.concatenate([q_decayed, attn[ci, h]], axis=1)
        rhs = jnp.concatenate([states[h].astype(BF16), v_ref[rows, vs]], axis=0)
        o_ref[rows, vs] = _dot(lhs, rhs).astype(o_ref.dtype)
        states[h] = _decay_columns(factor(0, ci, ks)) * states[h] + update[ci, h]
    for h in range(GLA_HEADS):
        s_ref[h] = states[h]


def _swa_tile(i, sinks_ref, q_ref, k_ref, v_ref, o_ref, kbuf, vbuf):
    ctx = _swa_setup(i, k_ref, v_ref, kbuf, vbuf)
    items = [(kvh, blk) for kvh in range(SWA_KV_HEADS) for blk in range(ROW_TILE // SWA_BLOCK)]
    scores = [_swa_scores(ctx, q_ref, kvh, blk) for kvh, blk in items]
    for s2, (kvh, blk) in zip(scores, items):
        _swa_attend(ctx, i, sinks_ref, o_ref, s2, kvh, blk)
    _swa_carry(kbuf, vbuf)


def _mixer_body(sinks_ref, qa_ref, ka_ref, va_ref, qin_ref, kin_ref, vb_ref, fac_ref,
                x_ref, rb_ref, ga_ref, gb_ref, og_ref, wa_ref, wb_ref, wo_ref,
                o_ref, kbuf, vbuf, s_ref, oa_scr, ob_scr):
    i = pl.program_id(1)
    _gla_tile(i, qin_ref, kin_ref, vb_ref, fac_ref, ob_scr, s_ref)
    _swa_tile(i, sinks_ref, qa_ref, ka_ref, va_ref, oa_scr, kbuf, vbuf)
    _outproj_body(x_ref, oa_scr, ob_scr, rb_ref, ga_ref, gb_ref, og_ref, wa_ref, wb_ref, wo_ref, o_ref)


def _swa_scratch():
    return [pltpu.VMEM((ROW_TILE + SWA_BLOCK, LANES), BF16), pltpu.VMEM((ROW_TILE + SWA_BLOCK, LANES), BF16)]


def _mixer(qa, ka, va, sinks, qin, kin, vb, fac, x, rb, ga, gb, out_gain, wa, wb, wo, batch, seq):
    nt = seq // ROW_TILE
    tok = lambda b, i, *_: (b * nt + i, 0)
    spec = lambda width: pl.BlockSpec((ROW_TILE, width), tok)
    fixed = lambda shape: _resident(shape, lambda b, i, *_: (0,) * len(shape))
    return pl.pallas_call(
        _mixer_body,
        grid_spec=pltpu.PrefetchScalarGridSpec(
            num_scalar_prefetch=1,
            grid=(batch, nt),
            in_specs=[spec(SWA_Q_W), spec(SWA_KV_W), spec(SWA_KV_W),
                      spec(GLA_K_W), spec(GLA_K_W), spec(GLA_V_W),
                      pl.BlockSpec((1, GLA_FACTOR_ROWS, GLA_K_W), lambda b, i, *_: (b * nt + i, 0, 0)),
                      spec(D_MODEL), spec(GLA_V_W), spec(D_MODEL), spec(D_MODEL),
                      fixed((1, GLA_V_W)), fixed((SWA_Q_W, D_MODEL)), fixed((GLA_V_W, D_MODEL)),
                      fixed((D_MODEL, D_MODEL))],
            out_specs=spec(D_MODEL),
            scratch_shapes=_swa_scratch() + [pltpu.VMEM((GLA_HEADS, GLA_DK, GLA_DV), F32),
                                             pltpu.VMEM((ROW_TILE, SWA_Q_W), BF16),
                                             pltpu.VMEM((ROW_TILE, GLA_V_W), BF16)],
        ),
        out_shape=jax.ShapeDtypeStruct((batch * seq, D_MODEL), F32),
        compiler_params=_params(2),
        name="mixer",
    )(sinks, qa, ka, va, qin, kin, vb, fac, x, rb, ga, gb, out_gain, wa, wb, wo)


def _swa_body(sinks_ref, q_ref, k_ref, v_ref, o_ref, kbuf, vbuf):
    _swa_tile(pl.program_id(1), sinks_ref, q_ref, k_ref, v_ref, o_ref, kbuf, vbuf)


def _swa(qa, ka, va, sinks, batch, seq):
    nt = seq // ROW_TILE
    tok = lambda b, i, *_: (b * nt + i, 0)
    return pl.pallas_call(
        _swa_body,
        grid_spec=pltpu.PrefetchScalarGridSpec(
            num_scalar_prefetch=1,
            grid=(batch, nt),
            in_specs=[
                pl.BlockSpec((ROW_TILE, SWA_Q_W), tok),
                pl.BlockSpec((ROW_TILE, SWA_KV_W), tok),
                pl.BlockSpec((ROW_TILE, SWA_KV_W), tok),
            ],
            out_specs=pl.BlockSpec((ROW_TILE, SWA_Q_W), tok),
            scratch_shapes=_swa_scratch(),
        ),
        out_shape=jax.ShapeDtypeStruct((batch * seq, SWA_Q_W), BF16),
        compiler_params=_params(2),
        name="swa",
    )(sinks, qa, ka, va)


def _gla_exact_body(q_ref, k_ref, v_ref, b_ref, o_ref, s_ref, kf_ref, bf_ref):
    i = pl.program_id(1)
    c = GLA_CHUNK

    @pl.when(i == 0)
    def _():
        s_ref[...] = jnp.zeros(s_ref.shape, F32)

    t_col = lax.broadcasted_iota(jnp.int32, (c, 1), 0)
    s_lane = lax.broadcasted_iota(jnp.int32, (1, c), 1)

    for h in range(GLA_HEADS):
        ks = slice(h * GLA_DK, (h + 1) * GLA_DK)
        vs = slice(h * GLA_DV, (h + 1) * GLA_DV)

        def chunk_step(ci, carry, h=h, ks=ks, vs=vs):
            rows = pl.ds(pl.multiple_of(ci * c, c), c)
            b = b_ref[rows, ks]
            q = q_ref[rows, ks].astype(F32)
            k = k_ref[rows, ks].astype(F32)
            v = v_ref[rows, vs]
            kf_ref[...] = k
            bf_ref[...] = b

            def key_column(s, attn):
                pair = jnp.exp(jnp.minimum(b - bf_ref[pl.ds(s, 1), :], 0.0))
                col = jnp.sum(q * pair * kf_ref[pl.ds(s, 1), :], axis=-1, keepdims=True)
                col = jnp.where(t_col >= s, col, 0.0)
                return attn + col * (s_lane == s).astype(F32)

            attn = lax.fori_loop(0, c, key_column, jnp.zeros((c, c), F32))
            b_last = b[c - 1:c, :]
            state = s_ref[h]
            o = _dot((q * jnp.exp(b)).astype(BF16), state.astype(BF16)) + _dot(attn.astype(BF16), v)
            o_ref[rows, vs] = o.astype(o_ref.dtype)
            s_ref[h] = (_decay_columns(jnp.exp(b_last)) * state
                        + _dot_tn((k * jnp.exp(b_last - b)).astype(BF16), v))
            return carry

        lax.fori_loop(0, CHUNKS_PER_TILE, chunk_step, 0)


def _gla_exact(q, k, vb, bcum, batch, seq):
    nt = seq // ROW_TILE
    tok = lambda b, i: (b * nt + i, 0)
    kspec = pl.BlockSpec((ROW_TILE, GLA_K_W), tok)
    vspec = pl.BlockSpec((ROW_TILE, GLA_V_W), tok)
    return pl.pallas_call(
        _gla_exact_body,
        grid=(batch, nt),
        in_specs=[kspec, kspec, vspec, kspec],
        out_specs=vspec,
        out_shape=jax.ShapeDtypeStruct((batch * seq, GLA_V_W), BF16),
        scratch_shapes=[pltpu.VMEM((GLA_HEADS, GLA_DK, GLA_DV), F32),
                        pltpu.VMEM((GLA_CHUNK, GLA_DK), F32), pltpu.VMEM((GLA_CHUNK, GLA_DK), F32)],
        compiler_params=_params(2),
        name="gla_exact",
    )(q, k, vb, bcum)


def _outproj_body(x_ref, oa_ref, ob_ref, rb_ref, ga_ref, gb_ref, og_ref, wa_ref, wb_ref, wo_ref, o_ref):
    quarter = D_MODEL // GLA_HEADS
    ya = []
    gated = []
    for h in range(GLA_HEADS):
        vs = slice(h * GLA_DV, (h + 1) * GLA_DV)
        ya.append(_dot(oa_ref[...], wa_ref[:, h * quarter:(h + 1) * quarter]))
        on = _rms(ob_ref[:, vs].astype(F32), og_ref[:, vs])
        gated.append((on * rb_ref[:, vs].astype(F32)).astype(BF16))
    gated = jnp.concatenate(gated, axis=1)
    merged = []
    for j in range(GLA_HEADS):
        cs = slice(j * quarter, (j + 1) * quarter)
        yb = _dot(gated, wb_ref[:, cs])
        merged.append((ga_ref[:, cs].astype(F32) * ya[j] + gb_ref[:, cs].astype(F32) * yb).astype(BF16))
    o_ref[...] = x_ref[...] + _dot(jnp.concatenate(merged, axis=1), wo_ref[...])


def _outproj(x, oa, ob, rb, ga, gb, out_gain, wa, wb, wo):
    n = x.shape[0]
    spec = lambda width: pl.BlockSpec((ROW_TILE, width), lambda i: (i, 0))
    (out,), _ = _dense_call(
        _outproj_body, "outproj", n,
        [spec(D_MODEL), spec(SWA_Q_W), spec(GLA_V_W), spec(GLA_V_W), spec(D_MODEL), spec(D_MODEL),
         _whole((1, GLA_V_W)), _whole((SWA_Q_W, D_MODEL)), _whole((GLA_V_W, D_MODEL)),
         _whole((D_MODEL, D_MODEL))],
        (x, oa, ob, rb, ga, gb, out_gain, wa, wb, wo),
        [spec(D_MODEL)], [jax.ShapeDtypeStruct((n, D_MODEL), F32)])
    return out


def _rope_tables(seq):
    inv_freq = ROPE_THETA ** (-jnp.arange(0, SWA_HEAD_DIM, 2, dtype=F32) / SWA_HEAD_DIM)
    ang = jnp.arange(seq, dtype=F32)[:, None] * inv_freq[None, :]
    cos, sin = jnp.cos(ang), jnp.sin(ang)
    reps = LANES // SWA_HEAD_DIM
    cos_t = jnp.tile(jnp.concatenate([cos, cos], axis=-1), (1, reps))
    sin_t = jnp.tile(jnp.concatenate([-sin, sin], axis=-1), (1, reps))
    return cos_t, sin_t


def kernel(x, ffn1_norm, ffn1_w_gate, ffn1_w_up, ffn1_w_down, mix_norm, w_in, swa_q_norm, swa_k_norm, swa_sinks, gla_w_gate, gla_gate_bias, gla_out_norm, w_proj_a, w_proj_b, w_out, ffn2_norm, ffn2_w_gate, ffn2_w_up, ffn2_w_down):
    batch, seq, d = x.shape
    assert d == D_MODEL and seq % ROW_TILE == 0
    n = batch * seq
    cos_t, sin_t = _rope_tables(seq)

    bf = lambda w: w.astype(BF16)
    wgate_p = jnp.concatenate(
        [gla_w_gate, jnp.zeros((DEPTH, GLR_PAD - GLA_GATE_RANK, GLA_K_W), F32)], axis=1).astype(BF16)
    head_id = jnp.arange(SWA_Q_W, dtype=jnp.int32) // SWA_HEAD_DIM
    head_ones = (head_id[:, None] == head_id[None, :]).astype(BF16)
    reps = LANES // SWA_HEAD_DIM
    q_gain = jnp.tile(swa_q_norm, (1, reps))
    k_gain = jnp.tile(swa_k_norm, (1, reps))
    row = lambda g, l: g[l:l + 1]

    w_in_t = jnp.transpose(w_in, (0, 2, 1))
    ffn1_w = (ffn1_w_gate, ffn1_w_up, ffn1_w_down)
    ffn2_w = (ffn2_w_gate, ffn2_w_up, ffn2_w_down)
    mix_w = (w_proj_a, w_proj_b, w_out)
    at = lambda ws, l: [(w, l) for w in ws]

    f1 = tuple(bf(w[0]) for w in ffn1_w)
    h = x.reshape(n, D_MODEL)
    for l in range(DEPTH):
        convs = [(w_in_t, l)] + at(ffn2_w, l) + (at(mix_w, 0) if l == 0 else [])
        h, conv = _ffn(h, row(ffn1_norm, l), *f1, convs=convs)
        w_t, f2 = conv[0], conv[1:4]
        if l == 0:
            wa, wb, wo = conv[4:]
        qa, ka, va, qin, kin, vb, rb, ga, gb, qraw, kraw, fac, bcum = _inproj(
            h, row(mix_norm, l), w_t, cos_t, sin_t, row(q_gain, l), row(k_gain, l), wgate_p[l],
            row(gla_gate_bias, l), head_ones, seq)
        tail = (h, rb, ga, gb, row(gla_out_norm, l), wa, wb, wo)
        h = lax.cond(
            jnp.min(fac[:, :CHUNKS_PER_TILE]) >= GLA_MIN_CHUNK_DECAY,
            lambda *ops: _mixer(*ops[:16], batch, seq),
            lambda *ops: _outproj(ops[8], _swa(*ops[:4], batch, seq),
                                  _gla_exact(ops[16], ops[17], ops[6], ops[18], batch, seq), *ops[9:16]),
            qa, ka, va, swa_sinks[l], qin, kin, vb, fac, *tail, qraw, kraw, bcum)
        convs = [] if l + 1 == DEPTH else at(ffn1_w, l + 1) + at(mix_w, l + 1)
        h, conv = _ffn(h, row(ffn2_norm, l), *f2, convs=convs)
        if convs:
            f1, (wa, wb, wo) = conv[:3], conv[3:]
    return h.reshape(batch, seq, D_MODEL)
```

```python
import jax
import jax.numpy as jnp
from jax import lax
from jax.experimental import pallas as pl
from jax.experimental.pallas import tpu as pltpu

F32 = jnp.float32
BF16 = jnp.bfloat16

D_MODEL = 1024
DEPTH = 4
D_FF = 2816
EPS = 1e-6

SWA_HEADS = 8
SWA_KV_HEADS = 2
SWA_GROUP = SWA_HEADS // SWA_KV_HEADS
SWA_HEAD_DIM = 64
SWA_BLOCK = 128
ROPE_THETA = 10000.0
SWA_Q_W = SWA_HEADS * SWA_HEAD_DIM
SWA_KV_W = SWA_KV_HEADS * SWA_HEAD_DIM

GLA_HEADS = 4
GLA_DK = 128
GLA_DV = 256
GLA_GATE_RANK = 16
GLA_TAU = 16.0
GLA_K_W = GLA_HEADS * GLA_DK
GLA_V_W = GLA_HEADS * GLA_DV

IN_SPLITS = (SWA_Q_W, SWA_KV_W, SWA_KV_W, GLA_K_W, GLA_K_W, GLA_V_W, GLA_V_W, GLA_GATE_RANK, D_MODEL, D_MODEL)

LANES = 128
GLR_PAD = LANES
IN_COLS = sum(IN_SPLITS)
IN_OFF = tuple(sum(IN_SPLITS[:j]) for j in range(len(IN_SPLITS)))
(P_QA, P_KA, P_VA, P_QB, P_KB, P_VB, P_RB, P_GLR, P_GA, P_GB) = range(len(IN_SPLITS))

ROW_TILE = 512
FFN_TILE = 1024
FFN_SUB = 256
GLA_CHUNK = 128
CHUNKS_PER_TILE = ROW_TILE // GLA_CHUNK
GLA_FACTOR_ROWS = 3 * CHUNKS_PER_TILE
VMEM_LIMIT = 56 * 1024 * 1024
GLA_MIN_CHUNK_DECAY = 8.7e-27


def _rms(x, gain):
    ms = jnp.mean(x * x, axis=-1, keepdims=True)
    return x * lax.rsqrt(ms + EPS) * gain


def _sigmoid(x):
    return 0.5 * jnp.tanh(0.5 * x) + 0.5


def _dot(a, b):
    return jnp.dot(a, b, preferred_element_type=F32)


def _dot_nt(a, b):
    return lax.dot_general(a, b, (((1,), (1,)), ((), ())), preferred_element_type=F32)


def _dot_tn(a, b):
    return lax.dot_general(a, b, (((0,), (0,)), ((), ())), preferred_element_type=F32)


def _hi_lo(x):
    hi = x.astype(BF16)
    return hi, (x - hi.astype(F32)).astype(BF16)


def _resident(shape, index_map):
    return pl.BlockSpec(shape, index_map, pipeline_mode=pl.Buffered(1))


def _params(n_axes):
    return pltpu.CompilerParams(dimension_semantics=("arbitrary",) * n_axes,
                                vmem_limit_bytes=VMEM_LIMIT)


BF16_SUBLANES = 16


def _dense_call(body, name, n, in_specs, args, out_specs, out_shapes, convs=None, tile=None):
    nsteps = n // (tile or ROW_TILE)
    n_in, n_out, n_conv = len(in_specs), len(out_specs), len(convs or ())
    in_specs, out_specs, out_shapes, args = list(in_specs), list(out_specs), list(out_shapes), list(args)
    for w, layer in convs or ():
        _, rows, cols = w.shape
        nblk = max(b for b in range(1, nsteps + 1) if rows % (b * BF16_SUBLANES) == 0)
        last = nblk - 1
        in_specs.append(pl.BlockSpec((None, rows // nblk, cols),
                                     lambda i, layer=layer, last=last: (layer, jnp.minimum(i, last), 0)))
        out_specs.append(pl.BlockSpec((rows // nblk, cols), lambda i, last=last: (jnp.minimum(i, last), 0)))
        out_shapes.append(jax.ShapeDtypeStruct((rows, cols), BF16))
        args.append(w)

    def wrapped(*refs):
        ins, srcs = refs[:n_in], refs[n_in:n_in + n_conv]
        outs, dsts = refs[n_in + n_conv:n_in + n_conv + n_out], refs[n_in + n_conv + n_out:]

        def cast_weights():
            for s, d in zip(srcs, dsts):
                d[...] = s[...].astype(BF16)

        if convs is None:
            body(*ins, *outs)
        else:
            body(cast_weights, *ins, *outs)

    res = pl.pallas_call(
        wrapped, grid=(nsteps,), in_specs=in_specs, out_specs=out_specs, out_shape=out_shapes,
        compiler_params=_params(1), name=name)(*args)
    return res[:n_out], res[n_out:]


def _whole(shape):
    return _resident(shape, lambda i: (0,) * len(shape))


def _ffn_body(cast_weights, x_ref, gain_ref, wg_ref, wu_ref, wd_ref, o_ref):
    nsub = FFN_TILE // FFN_SUB
    rows = [slice(j * FFN_SUB, (j + 1) * FFN_SUB) for j in range(nsub)]

    def up(j):
        h = _rms(x_ref[rows[j], :], gain_ref[...]).astype(BF16)
        return _dot(h, wg_ref[...]), _dot(h, wu_ref[...])

    def down(j, g, u):
        a = (g * _sigmoid(g) * u).astype(BF16)
        o_ref[rows[j], :] = x_ref[rows[j], :] + 0.5 * _dot(a, wd_ref[...])

    pending = up(0)
    for j in range(nsub):
        nxt = up(j + 1) if j + 1 < nsub else None
        if j == 0:
            cast_weights()
        down(j, *pending)
        pending = nxt


def _ffn(x, gain, wg, wu, wd, convs=()):
    n = x.shape[0]
    row = pl.BlockSpec((FFN_TILE, D_MODEL), lambda i: (i, 0))
    (out,), converted = _dense_call(
        _ffn_body, "ffn", n,
        [row, _whole((1, D_MODEL)), _whole((D_MODEL, D_FF)), _whole((D_MODEL, D_FF)), _whole((D_FF, D_MODEL))],
        (x, gain, wg, wu, wd), [row], [jax.ShapeDtypeStruct((n, D_MODEL), F32)], convs, tile=FFN_TILE)
    return out, converted


def _head_norm_rope(x, ssq, gain, cos, sin_signed, lane):
    xn = x * lax.rsqrt(ssq * (1.0 / SWA_HEAD_DIM) + EPS) * gain
    half = SWA_HEAD_DIM // 2
    partner = jnp.where((lane % SWA_HEAD_DIM) < half,
                        pltpu.roll(xn, LANES - half, 1), pltpu.roll(xn, half, 1))
    return xn * cos + partner * sin_signed


def _inproj_body(x_ref, gain_ref, w_ref, cos_ref, sin_ref, qg_ref, kg_ref, wgate_ref, bias_ref,
                 ones_ref, qa_ref, ka_ref, va_ref, qin_ref, kin_ref, vb_ref, rb_ref,
                 ga_ref, gb_ref, qraw_ref, kraw_ref, fac_ref, bcum_ref):
    h = _rms(x_ref[...], gain_ref[...]).astype(BF16)
    half = D_MODEL // 2
    lane = lax.broadcasted_iota(jnp.int32, (1, LANES), 1)

    def proj(which, lo=0, width=None):
        off = IN_OFF[which] + lo
        width = IN_SPLITS[which] if width is None else width
        return _dot_nt(h, w_ref[off:off + width, :])

    narrow = proj(P_QA, 0, IN_OFF[P_QB])
    tail = proj(P_RB, half, half + LANES)
    glr = jnp.where(lane < GLA_GATE_RANK, tail[:, half:], 0.0).astype(BF16)
    logits = _dot(glr, wgate_ref[...]) + bias_ref[...]
    r = tail[:, :half]
    rb_ref[:, half:] = (r * _sigmoid(r)).astype(BF16)
    q = proj(P_QB) * GLA_DK ** -0.5
    la = (jnp.minimum(logits, 0.0) - jnp.log1p(jnp.exp(-jnp.abs(logits)))) * (1.0 / GLA_TAU)
    la_hi, la_lo = _hi_lo(la)
    k = proj(P_KB)
    qraw_ref[...] = q.astype(BF16)
    kraw_ref[...] = k.astype(BF16)

    c = GLA_CHUNK
    t_i = lax.broadcasted_iota(jnp.int32, (c, c), 0)
    s_i = lax.broadcasted_iota(jnp.int32, (c, c), 1)
    tril = (s_i <= t_i).astype(BF16)
    cums = []
    for ci in range(CHUNKS_PER_TILE):
        rows = slice(ci * c, (ci + 1) * c)
        cums.append(_dot(tril, la_hi[rows]) + _dot(tril, la_lo[rows]))

    def decay_products(ci):
        rows = slice(ci * c, (ci + 1) * c)
        b = cums[ci]
        b_last = b[c - 1:c, :]
        b_mid = b[c // 2 - 1:c // 2, :]
        qin_ref[rows, :] = (q[rows] * jnp.exp(b - b_mid)).astype(BF16)
        kin_ref[rows, :] = (k[rows] * jnp.exp(b_mid - b)).astype(BF16)
        fac_ref[0, ci:ci + 1, :] = jnp.exp(b_last)
        fac_ref[0, CHUNKS_PER_TILE + ci:CHUNKS_PER_TILE + ci + 1, :] = jnp.exp(b_mid)
        fac_ref[0, 2 * CHUNKS_PER_TILE + ci:2 * CHUNKS_PER_TILE + ci + 1, :] = jnp.exp(b_last - b_mid)
        bcum_ref[rows, :] = b

    cos, sin = cos_ref[...], sin_ref[...]
    qa = narrow[:, :SWA_Q_W]
    ssq_q = _dot((qa * qa).astype(BF16), ones_ref[...])

    def rope_q(col):
        cs = slice(col * LANES, (col + 1) * LANES)
        qc = _head_norm_rope(qa[:, cs], ssq_q[:, cs], qg_ref[...], cos, sin, lane)
        qa_ref[:, cs] = (qc * SWA_HEAD_DIM ** -0.5).astype(BF16)

    vb_ref[:, :half] = proj(P_VB, 0, half).astype(BF16)
    decay_products(0)
    r = proj(P_RB, 0, half)
    decay_products(1)
    rb_ref[:, :half] = (r * _sigmoid(r)).astype(BF16)
    decay_products(2)
    for col, (which, ref, lo) in enumerate(((P_GA, ga_ref, 0), (P_GA, ga_ref, half),
                                             (P_GB, gb_ref, 0), (P_GB, gb_ref, half))):
        ref[:, lo:lo + half] = _sigmoid(proj(which, lo, half)).astype(BF16)
        if col == 0:
            decay_products(3)
        rope_q(col)
    ka = narrow[:, IN_OFF[P_KA]:IN_OFF[P_VA]]
    ssq_k = _dot((ka * ka).astype(BF16), ones_ref[0:SWA_KV_W, 0:SWA_KV_W])
    ka_ref[...] = _head_norm_rope(ka, ssq_k, kg_ref[...], cos, sin, lane).astype(BF16)
    va_ref[...] = narrow[:, IN_OFF[P_VA]:IN_OFF[P_QB]].astype(BF16)
    vb_ref[:, half:] = proj(P_VB, half, half).astype(BF16)


def _inproj(x, gain, w, cos, sin, q_gain, k_gain, wgate, bias, head_ones, seq):
    n = x.shape[0]
    nt = n // ROW_TILE
    tiles_per_seq = seq // ROW_TILE
    row = lambda i: (i, 0)
    pos = lambda i: (i % tiles_per_seq, 0)
    widths = (SWA_Q_W, SWA_KV_W, SWA_KV_W, GLA_K_W, GLA_K_W,
              GLA_V_W, GLA_V_W, D_MODEL, D_MODEL, GLA_K_W, GLA_K_W)
    outs, _ = _dense_call(
        _inproj_body, "inproj", n,
        [pl.BlockSpec((ROW_TILE, D_MODEL), row), _whole((1, D_MODEL)), _whole((IN_COLS, D_MODEL)),
         pl.BlockSpec((ROW_TILE, LANES), pos), pl.BlockSpec((ROW_TILE, LANES), pos),
         _whole((1, LANES)), _whole((1, LANES)), _whole((GLR_PAD, GLA_K_W)), _whole((1, GLA_K_W)),
         _whole((SWA_Q_W, SWA_Q_W))],
        (x, gain, w, cos, sin, q_gain, k_gain, wgate, bias, head_ones),
        [pl.BlockSpec((ROW_TILE, width), row) for width in widths]
        + [pl.BlockSpec((1, GLA_FACTOR_ROWS, GLA_K_W), lambda i: (i, 0, 0)),
           pl.BlockSpec((ROW_TILE, GLA_K_W), row)],
        [jax.ShapeDtypeStruct((n, width), BF16) for width in widths]
        + [jax.ShapeDtypeStruct((nt, GLA_FACTOR_ROWS, GLA_K_W), F32),
           jax.ShapeDtypeStruct((n, GLA_K_W), F32)])
    return outs


def _swa_setup(i, k_ref, v_ref, kbuf, vbuf):
    lane = lax.broadcasted_iota(jnp.int32, (1, LANES), 1)
    first_half = lane < SWA_HEAD_DIM

    @pl.when(i == 0)
    def _():
        kbuf[0:SWA_BLOCK, :] = jnp.zeros((SWA_BLOCK, LANES), BF16)
        vbuf[0:SWA_BLOCK, :] = jnp.zeros((SWA_BLOCK, LANES), BF16)

    kbuf[SWA_BLOCK:, :] = k_ref[...]
    vbuf[SWA_BLOCK:, :] = v_ref[...]
    kall = kbuf[...]
    vall = vbuf[...]
    kswap = pltpu.roll(kall, SWA_HEAD_DIM, 1)
    vswap = pltpu.roll(vall, SWA_HEAD_DIM, 1)
    kdup = [jnp.where(first_half, kall, kswap), jnp.where(first_half, kswap, kall)]
    vdup = [jnp.where(first_half, vall, vswap), jnp.where(first_half, vswap, vall)]

    rows = SWA_GROUP * SWA_BLOCK
    t_idx = lax.broadcasted_iota(jnp.int32, (rows, SWA_BLOCK), 0) % SWA_BLOCK
    c_idx = lax.broadcasted_iota(jnp.int32, (rows, SWA_BLOCK), 1)
    use_cur = c_idx <= t_idx
    row_head = lax.broadcasted_iota(jnp.int32, (rows, 1), 0) // SWA_BLOCK
    return first_half, kdup, vdup, use_cur, row_head


def _swa_scores(ctx, q_ref, kvh, blk):
    first_half, kdup = ctx[0], ctx[1]
    r0 = blk * SWA_BLOCK
    parts = []
    for g in range(SWA_GROUP):
        head = kvh * SWA_GROUP + g
        qc = q_ref[r0:r0 + SWA_BLOCK, (head // 2) * LANES:(head // 2 + 1) * LANES]
        keep = first_half if head % 2 == 0 else jnp.logical_not(first_half)
        parts.append(jnp.where(keep, qc, jnp.zeros_like(qc)))
    return _dot_nt(jnp.concatenate(parts, axis=0), kdup[kvh][r0:r0 + 2 * SWA_BLOCK, :])


def _swa_attend(ctx, i, sinks_ref, o_ref, s2, kvh, blk):
    first_half, _, vdup, use_cur, row_head = ctx
    rows = SWA_GROUP * SWA_BLOCK
    sink = jnp.zeros((rows, 1), F32)
    for g in range(SWA_GROUP):
        sink = jnp.where(row_head == g, sinks_ref[kvh * SWA_GROUP + g], sink)
    r0 = blk * SWA_BLOCK
    s_prev = s2[:, :SWA_BLOCK]
    if blk == 0:
        s_prev = s_prev + jnp.where(i > 0, 0.0, -jnp.inf)
    s = jnp.where(use_cur, s2[:, SWA_BLOCK:], s_prev)
    m = jnp.maximum(jnp.max(s, axis=-1, keepdims=True), sink)
    p = jnp.exp(s - m)
    denom = jnp.sum(p, axis=-1, keepdims=True) + jnp.exp(sink - m)
    p2 = jnp.concatenate([jnp.where(use_cur, 0.0, p), jnp.where(use_cur, p, 0.0)], axis=1)
    vals = vdup[kvh][r0:r0 + 2 * SWA_BLOCK, :]
    o = _dot(p2.astype(BF16), vals) / denom
    for pair in range(SWA_GROUP // 2):
        a = o[(2 * pair) * SWA_BLOCK:(2 * pair + 1) * SWA_BLOCK, :]
        b = o[(2 * pair + 1) * SWA_BLOCK:(2 * pair + 2) * SWA_BLOCK, :]
        col = (kvh * SWA_GROUP) // 2 + pair
        o_ref[r0:r0 + SWA_BLOCK, col * LANES:(col + 1) * LANES] = (
            jnp.where(first_half, a, b).astype(o_ref.dtype))


def _swa_carry(kbuf, vbuf):
    kbuf[0:SWA_BLOCK, :] = kbuf[ROW_TILE:ROW_TILE + SWA_BLOCK, :]
    vbuf[0:SWA_BLOCK, :] = vbuf[ROW_TILE:ROW_TILE + SWA_BLOCK, :]


def _decay_columns(decay_row):
    cols = jnp.transpose(jnp.broadcast_to(decay_row, (GLA_CHUNK, GLA_DK)))
    return jnp.concatenate([cols] * (GLA_DV // GLA_CHUNK), axis=1)


def _gla_tile(i, qin_ref, kin_ref, v_ref, fac_ref, o_ref, s_ref):
    c = GLA_CHUNK

    @pl.when(i == 0)
    def _():
        s_ref[...] = jnp.zeros(s_ref.shape, F32)

    t_i = lax.broadcasted_iota(jnp.int32, (c, c), 0)
    s_i = lax.broadcasted_iota(jnp.int32, (c, c), 1)
    causal = s_i <= t_i
    items = [(ci, h) for ci in range(CHUNKS_PER_TILE) for h in range(GLA_HEADS)]
    window = lambda ci, h: (slice(ci * c, (ci + 1) * c), slice(h * GLA_DK, (h + 1) * GLA_DK),
                            slice(h * GLA_DV, (h + 1) * GLA_DV))
    factor = lambda kind, ci, ks: fac_ref[0, kind * CHUNKS_PER_TILE + ci:kind * CHUNKS_PER_TILE + ci + 1, ks]

    attn, update = {}, {}
    for ci, h in items:
        rows, ks, vs = window(ci, h)
        scores = _dot_nt(qin_ref[rows, ks], kin_ref[rows, ks])
        attn[ci, h] = jnp.where(causal, scores, 0.0).astype(BF16)
        k_state = (kin_ref[rows, ks].astype(F32) * factor(2, ci, ks)).astype(BF16)
        update[ci, h] = _dot_tn(k_state, v_ref[rows, vs])

    states = [s_ref[h] for h in range(GLA_HEADS)]
    for ci, h in items:
        rows, ks, vs = window(ci, h)
        q_decayed = (qin_ref[rows, ks].astype(F32) * factor(1, ci, ks)).astype(BF16)
        lhs = jnp.concatenate([q_decayed, attn[ci, h]], axis=1)
        rhs = jnp.concatenate([states[h].astype(BF16), v_ref[rows, vs]], axis=0)
        o_ref[rows, vs] = _dot(lhs, rhs).astype(o_ref.dtype)
        states[h] = _decay_columns(factor(0, ci, ks)) * states[h] + update[ci, h]
    for h in range(GLA_HEADS):
        s_ref[h] = states[h]


def _swa_tile(i, sinks_ref, q_ref, k_ref, v_ref, o_ref, kbuf, vbuf):
    ctx = _swa_setup(i, k_ref, v_ref, kbuf, vbuf)
    items = [(kvh, blk) for kvh in range(SWA_KV_HEADS) for blk in range(ROW_TILE // SWA_BLOCK)]
    scores = [_swa_scores(ctx, q_ref, kvh, blk) for kvh, blk in items]
    for s2, (kvh, blk) in zip(scores, items):
        _swa_attend(ctx, i, sinks_ref, o_ref, s2, kvh, blk)
    _swa_carry(kbuf, vbuf)


def _mixer_body(sinks_ref, qa_ref, ka_ref, va_ref, qin_ref, kin_ref, vb_ref, fac_ref,
                x_ref, rb_ref, ga_ref, gb_ref, og_ref, wa_ref, wb_ref, wo_ref,
                o_ref, kbuf, vbuf, s_ref, oa_scr, ob_scr):
    i = pl.program_id(1)
    _gla_tile(i, qin_ref, kin_ref, vb_ref, fac_ref, ob_scr, s_ref)
    _swa_tile(i, sinks_ref, qa_ref, ka_ref, va_ref, oa_scr, kbuf, vbuf)
    _outproj_body(x_ref, oa_scr, ob_scr, rb_ref, ga_ref, gb_ref, og_ref, wa_ref, wb_ref, wo_ref, o_ref)


def _swa_scratch():
    return [pltpu.VMEM((ROW_TILE + SWA_BLOCK, LANES), BF16), pltpu.VMEM((ROW_TILE + SWA_BLOCK, LANES), BF16)]


def _mixer(qa, ka, va, sinks, qin, kin, vb, fac, x, rb, ga, gb, out_gain, wa, wb, wo, batch, seq):
    nt = seq // ROW_TILE
    tok = lambda b, i, *_: (b * nt + i, 0)
    spec = lambda width: pl.BlockSpec((ROW_TILE, width), tok)
    fixed = lambda shape: _resident(shape, lambda b, i, *_: (0,) * len(shape))
    return pl.pallas_call(
        _mixer_body,
        grid_spec=pltpu.PrefetchScalarGridSpec(
            num_scalar_prefetch=1,
            grid=(batch, nt),
            in_specs=[spec(SWA_Q_W), spec(SWA_KV_W), spec(SWA_KV_W),
                      spec(GLA_K_W), spec(GLA_K_W), spec(GLA_V_W),
                      pl.BlockSpec((1, GLA_FACTOR_ROWS, GLA_K_W), lambda b, i, *_: (b * nt + i, 0, 0)),
                      spec(D_MODEL), spec(GLA_V_W), spec(D_MODEL), spec(D_MODEL),
                      fixed((1, GLA_V_W)), fixed((SWA_Q_W, D_MODEL)), fixed((GLA_V_W, D_MODEL)),
                      fixed((D_MODEL, D_MODEL))],
            out_specs=spec(D_MODEL),
            scratch_shapes=_swa_scratch() + [pltpu.VMEM((GLA_HEADS, GLA_DK, GLA_DV), F32),
                                             pltpu.VMEM((ROW_TILE, SWA_Q_W), BF16),
                                             pltpu.VMEM((ROW_TILE, GLA_V_W), BF16)],
        ),
        out_shape=jax.ShapeDtypeStruct((batch * seq, D_MODEL), F32),
        compiler_params=_params(2),
        name="mixer",
    )(sinks, qa, ka, va, qin, kin, vb, fac, x, rb, ga, gb, out_gain, wa, wb, wo)


def _swa_body(sinks_ref, q_ref, k_ref, v_ref, o_ref, kbuf, vbuf):
    _swa_tile(pl.program_id(1), sinks_ref, q_ref, k_ref, v_ref, o_ref, kbuf, vbuf)


def _swa(qa, ka, va, sinks, batch, seq):
    nt = seq // ROW_TILE
    tok = lambda b, i, *_: (b * nt + i, 0)
    return pl.pallas_call(
        _swa_body,
        grid_spec=pltpu.PrefetchScalarGridSpec(
            num_scalar_prefetch=1,
            grid=(batch, nt),
            in_specs=[
                pl.BlockSpec((ROW_TILE, SWA_Q_W), tok),
                pl.BlockSpec((ROW_TILE, SWA_KV_W), tok),
                pl.BlockSpec((ROW_TILE, SWA_KV_W), tok),
            ],
            out_specs=pl.BlockSpec((ROW_TILE, SWA_Q_W), tok),
            scratch_shapes=_swa_scratch(),
        ),
        out_shape=jax.ShapeDtypeStruct((batch * seq, SWA_Q_W), BF16),
        compiler_params=_params(2),
        name="swa",
    )(sinks, qa, ka, va)


def _gla_exact_body(q_ref, k_ref, v_ref, b_ref, o_ref, s_ref, kf_ref, bf_ref):
    i = pl.program_id(1)
    c = GLA_CHUNK

    @pl.when(i == 0)
    def _():
        s_ref[...] = jnp.zeros(s_ref.shape, F32)

    t_col = lax.broadcasted_iota(jnp.int32, (c, 1), 0)
    s_lane = lax.broadcasted_iota(jnp.int32, (1, c), 1)

    for h in range(GLA_HEADS):
        ks = slice(h * GLA_DK, (h + 1) * GLA_DK)
        vs = slice(h * GLA_DV, (h + 1) * GLA_DV)

        def chunk_step(ci, carry, h=h, ks=ks, vs=vs):
            rows = pl.ds(pl.multiple_of(ci * c, c), c)
            b = b_ref[rows, ks]
            q = q_ref[rows, ks].astype(F32)
            k = k_ref[rows, ks].astype(F32)
            v = v_ref[rows, vs]
            kf_ref[...] = k
            bf_ref[...] = b

            def key_column(s, attn):
                pair = jnp.exp(jnp.minimum(b - bf_ref[pl.ds(s, 1), :], 0.0))
                col = jnp.sum(q * pair * kf_ref[pl.ds(s, 1), :], axis=-1, keepdims=True)
                col = jnp.where(t_col >= s, col, 0.0)
                return attn + col * (s_lane == s).astype(F32)

            attn = lax.fori_loop(0, c, key_column, jnp.zeros((c, c), F32))
            b_last = b[c - 1:c, :]
            state = s_ref[h]
            o = _dot((q * jnp.exp(b)).astype(BF16), state.astype(BF16)) + _dot(attn.astype(BF16), v)
            o_ref[rows, vs] = o.astype(o_ref.dtype)
            s_ref[h] = (_decay_columns(jnp.exp(b_last)) * state
                        + _dot_tn((k * jnp.exp(b_last - b)).astype(BF16), v))
            return carry

        lax.fori_loop(0, CHUNKS_PER_TILE, chunk_step, 0)


def _gla_exact(q, k, vb, bcum, batch, seq):
    nt = seq // ROW_TILE
    tok = lambda b, i: (b * nt + i, 0)
    kspec = pl.BlockSpec((ROW_TILE, GLA_K_W), tok)
    vspec = pl.BlockSpec((ROW_TILE, GLA_V_W), tok)
    return pl.pallas_call(
        _gla_exact_body,
        grid=(batch, nt),
        in_specs=[kspec, kspec, vspec, kspec],
        out_specs=vspec,
        out_shape=jax.ShapeDtypeStruct((batch * seq, GLA_V_W), BF16),
        scratch_shapes=[pltpu.VMEM((GLA_HEADS, GLA_DK, GLA_DV), F32),
                        pltpu.VMEM((GLA_CHUNK, GLA_DK), F32), pltpu.VMEM((GLA_CHUNK, GLA_DK), F32)],
        compiler_params=_params(2),
        name="gla_exact",
    )(q, k, vb, bcum)


def _outproj_body(x_ref, oa_ref, ob_ref, rb_ref, ga_ref, gb_ref, og_ref, wa_ref, wb_ref, wo_ref, o_ref):
    quarter = D_MODEL // GLA_HEADS
    ya = []
    gated = []
    for h in range(GLA_HEADS):
        vs = slice(h * GLA_DV, (h + 1) * GLA_DV)
        ya.append(_dot(oa_ref[...], wa_ref[:, h * quarter:(h + 1) * quarter]))
        on = _rms(ob_ref[:, vs].astype(F32), og_ref[:, vs])
        gated.append((on * rb_ref[:, vs].astype(F32)).astype(BF16))
    gated = jnp.concatenate(gated, axis=1)
    merged = []
    for j in range(GLA_HEADS):
        cs = slice(j * quarter, (j + 1) * quarter)
        yb = _dot(gated, wb_ref[:, cs])
        merged.append((ga_ref[:, cs].astype(F32) * ya[j] + gb_ref[:, cs].astype(F32) * yb).astype(BF16))
    o_ref[...] = x_ref[...] + _dot(jnp.concatenate(merged, axis=1), wo_ref[...])


def _outproj(x, oa, ob, rb, ga, gb, out_gain, wa, wb, wo):
    n = x.shape[0]
    spec = lambda width: pl.BlockSpec((ROW_TILE, width), lambda i: (i, 0))
    (out,), _ = _dense_call(
        _outproj_body, "outproj", n,
        [spec(D_MODEL), spec(SWA_Q_W), spec(GLA_V_W), spec(GLA_V_W), spec(D_MODEL), spec(D_MODEL),
         _whole((1, GLA_V_W)), _whole((SWA_Q_W, D_MODEL)), _whole((GLA_V_W, D_MODEL)),
         _whole((D_MODEL, D_MODEL))],
        (x, oa, ob, rb, ga, gb, out_gain, wa, wb, wo),
        [spec(D_MODEL)], [jax.ShapeDtypeStruct((n, D_MODEL), F32)])
    return out


def _rope_tables(seq):
    inv_freq = ROPE_THETA ** (-jnp.arange(0, SWA_HEAD_DIM, 2, dtype=F32) / SWA_HEAD_DIM)
    ang = jnp.arange(seq, dtype=F32)[:, None] * inv_freq[None, :]
    cos, sin = jnp.cos(ang), jnp.sin(ang)
    reps = LANES // SWA_HEAD_DIM
    cos_t = jnp.tile(jnp.concatenate([cos, cos], axis=-1), (1, reps))
    sin_t = jnp.tile(jnp.concatenate([-sin, sin], axis=-1), (1, reps))
    return cos_t, sin_t


def kernel(x, ffn1_norm, ffn1_w_gate, ffn1_w_up, ffn1_w_down, mix_norm, w_in, swa_q_norm, swa_k_norm, swa_sinks, gla_w_gate, gla_gate_bias, gla_out_norm, w_proj_a, w_proj_b, w_out, ffn2_norm, ffn2_w_gate, ffn2_w_up, ffn2_w_down):
    batch, seq, d = x.shape
    assert d == D_MODEL and seq % ROW_TILE == 0 and (batch * seq) % FFN_TILE == 0
    n = batch * seq
    cos_t, sin_t = _rope_tables(seq)

    bf = lambda w: w.astype(BF16)
    wgate_p = jnp.concatenate(
        [gla_w_gate, jnp.zeros((DEPTH, GLR_PAD - GLA_GATE_RANK, GLA_K_W), F32)], axis=1).astype(BF16)
    head_id = jnp.arange(SWA_Q_W, dtype=jnp.int32) // SWA_HEAD_DIM
    head_ones = (head_id[:, None] == head_id[None, :]).astype(BF16)
    reps = LANES // SWA_HEAD_DIM
    q_gain = jnp.tile(swa_q_norm, (1, reps))
    k_gain = jnp.tile(swa_k_norm, (1, reps))
    row = lambda g, l: g[l:l + 1]

    w_in_t = jnp.transpose(w_in, (0, 2, 1))
    ffn1_w = (ffn1_w_gate, ffn1_w_up, ffn1_w_down)
    ffn2_w = (ffn2_w_gate, ffn2_w_up, ffn2_w_down)
    mix_w = (w_proj_a, w_proj_b, w_out)
    at = lambda ws, l: [(w, l) for w in ws]

    f1 = tuple(bf(w[0]) for w in ffn1_w)
    h = x.reshape(n, D_MODEL)
    for l in range(DEPTH):
        convs = [(w_in_t, l)] + at(ffn2_w, l) + (at(mix_w, 0) if l == 0 else [])
        h, conv = _ffn(h, row(ffn1_norm, l), *f1, convs=convs)
        w_t, f2 = conv[0], conv[1:4]
        if l == 0:
            wa, wb, wo = conv[4:]
        qa, ka, va, qin, kin, vb, rb, ga, gb, qraw, kraw, fac, bcum = _inproj(
            h, row(mix_norm, l), w_t, cos_t, sin_t, row(q_gain, l), row(k_gain, l), wgate_p[l],
            row(gla_gate_bias, l), head_ones, seq)
        tail = (h, rb, ga, gb, row(gla_out_norm, l), wa, wb, wo)
        h = lax.cond(
            jnp.min(fac[:, :CHUNKS_PER_TILE]) >= GLA_MIN_CHUNK_DECAY,
            lambda *ops: _mixer(*ops[:16], batch, seq),
            lambda *ops: _outproj(ops[8], _swa(*ops[:4], batch, seq),
                                  _gla_exact(ops[16], ops[17], ops[6], ops[18], batch, seq), *ops[9:16]),
            qa, ka, va, swa_sinks[l], qin, kin, vb, fac, *tail, qraw, kraw, bcum)
        convs = [] if l + 1 == DEPTH else at(ffn1_w, l + 1) + at(mix_w, l + 1)
        h, conv = _ffn(h, row(ffn2_norm, l), *f2, convs=convs)
        if convs:
            f1, (wa, wb, wo) = conv[:3], conv[3:]
    return h.reshape(batch, seq, D_MODEL)
```

```python
import jax
import jax.numpy as jnp
from jax import lax
from jax.experimental import pallas as pl
from jax.experimental.pallas import tpu as pltpu

F32 = jnp.float32
BF16 = jnp.bfloat16

D_MODEL = 1024
DEPTH = 4
D_FF = 2816
EPS = 1e-6

SWA_HEADS = 8
SWA_KV_HEADS = 2
SWA_GROUP = SWA_HEADS // SWA_KV_HEADS
SWA_HEAD_DIM = 64
SWA_BLOCK = 128
ROPE_THETA = 10000.0
SWA_Q_W = SWA_HEADS * SWA_HEAD_DIM
SWA_KV_W = SWA_KV_HEADS * SWA_HEAD_DIM

GLA_HEADS = 4
GLA_DK = 128
GLA_DV = 256
GLA_GATE_RANK = 16
GLA_TAU = 16.0
GLA_K_W = GLA_HEADS * GLA_DK
GLA_V_W = GLA_HEADS * GLA_DV

IN_SPLITS = (SWA_Q_W, SWA_KV_W, SWA_KV_W, GLA_K_W, GLA_K_W, GLA_V_W, GLA_V_W, GLA_GATE_RANK, D_MODEL, D_MODEL)

LANES = 128
GLR_PAD = LANES
IN_COLS = sum(IN_SPLITS)
IN_OFF = tuple(sum(IN_SPLITS[:j]) for j in range(len(IN_SPLITS)))
(P_QA, P_KA, P_VA, P_QB, P_KB, P_VB, P_RB, P_GLR, P_GA, P_GB) = range(len(IN_SPLITS))

ROW_TILE = 512
FFN_TILE = 1024
FFN_SUB = 256
GLA_CHUNK = 128
CHUNKS_PER_TILE = ROW_TILE // GLA_CHUNK
GLA_FACTOR_ROWS = 3 * CHUNKS_PER_TILE
VMEM_LIMIT = 56 * 1024 * 1024
GLA_MIN_CHUNK_DECAY = 8.7e-27


def _rms(x, gain):
    ms = jnp.mean(x * x, axis=-1, keepdims=True)
    return x * lax.rsqrt(ms + EPS) * gain


def _sigmoid(x):
    return 0.5 * jnp.tanh(0.5 * x) + 0.5


def _dot(a, b):
    return jnp.dot(a, b, preferred_element_type=F32)


def _dot_nt(a, b):
    return lax.dot_general(a, b, (((1,), (1,)), ((), ())), preferred_element_type=F32)


def _dot_tn(a, b):
    return lax.dot_general(a, b, (((0,), (0,)), ((), ())), preferred_element_type=F32)


def _hi_lo(x):
    hi = x.astype(BF16)
    return hi, (x - hi.astype(F32)).astype(BF16)


def _resident(shape, index_map):
    return pl.BlockSpec(shape, index_map, pipeline_mode=pl.Buffered(1))


def _params(n_axes):
    return pltpu.CompilerParams(dimension_semantics=("arbitrary",) * n_axes,
                                vmem_limit_bytes=VMEM_LIMIT)


BF16_SUBLANES = 16


def _dense_call(body, name, n, in_specs, args, out_specs, out_shapes, convs=None, tile=None):
    nsteps = n // (tile or ROW_TILE)
    n_in, n_out, n_conv = len(in_specs), len(out_specs), len(convs or ())
    in_specs, out_specs, out_shapes, args = list(in_specs), list(out_specs), list(out_shapes), list(args)
    for w, layer in convs or ():
        _, rows, cols = w.shape
        nblk = max(b for b in range(1, nsteps + 1) if rows % (b * BF16_SUBLANES) == 0)
        last = nblk - 1
        in_specs.append(pl.BlockSpec((None, rows // nblk, cols),
                                     lambda i, layer=layer, last=last: (layer, jnp.minimum(i, last), 0)))
        out_specs.append(pl.BlockSpec((rows // nblk, cols), lambda i, last=last: (jnp.minimum(i, last), 0)))
        out_shapes.append(jax.ShapeDtypeStruct((rows, cols), BF16))
        args.append(w)

    def wrapped(*refs):
        ins, srcs = refs[:n_in], refs[n_in:n_in + n_conv]
        outs, dsts = refs[n_in + n_conv:n_in + n_conv + n_out], refs[n_in + n_conv + n_out:]

        def cast_weights():
            for s, d in zip(srcs, dsts):
                d[...] = s[...].astype(BF16)

        if convs is None:
            body(*ins, *outs)
        else:
            body(cast_weights, *ins, *outs)

    res = pl.pallas_call(
        wrapped, grid=(nsteps,), in_specs=in_specs, out_specs=out_specs, out_shape=out_shapes,
        compiler_params=_params(1), name=name)(*args)
    return res[:n_out], res[n_out:]


def _whole(shape):
    return _resident(shape, lambda i: (0,) * len(shape))


def _ffn_body(cast_weights, x_ref, gain_ref, wg_ref, wu_ref, wd_ref, o_ref):
    nsub = FFN_TILE // FFN_SUB
    rows = [slice(j * FFN_SUB, (j + 1) * FFN_SUB) for j in range(nsub)]

    def up(j):
        h = _rms(x_ref[rows[j], :], gain_ref[...]).astype(BF16)
        return _dot(h, wg_ref[...]), _dot(h, wu_ref[...])

    def down(j, g, u):
        a = (g * _sigmoid(g) * u).astype(BF16)
        o_ref[rows[j], :] = x_ref[rows[j], :] + 0.5 * _dot(a, wd_ref[...])

    pending = up(0)
    for j in range(nsub):
        nxt = up(j + 1) if j + 1 < nsub else None
        if j == 0:
            cast_weights()
        down(j, *pending)
        pending = nxt


def _ffn(x, gain, wg, wu, wd, convs=()):
    n = x.shape[0]
    row = pl.BlockSpec((FFN_TILE, D_MODEL), lambda i: (i, 0))
    (out,), converted = _dense_call(
        _ffn_body, "ffn", n,
        [row, _whole((1, D_MODEL)), _whole((D_MODEL, D_FF)), _whole((D_MODEL, D_FF)), _whole((D_FF, D_MODEL))],
        (x, gain, wg, wu, wd), [row], [jax.ShapeDtypeStruct((n, D_MODEL), F32)], convs, tile=FFN_TILE)
    return out, converted


def _head_norm_rope(x, ssq, gain, cos, sin_signed, lane):
    xn = x * lax.rsqrt(ssq * (1.0 / SWA_HEAD_DIM) + EPS) * gain
    half = SWA_HEAD_DIM // 2
    partner = jnp.where((lane % SWA_HEAD_DIM) < half,
                        pltpu.roll(xn, LANES - half, 1), pltpu.roll(xn, half, 1))
    return xn * cos + partner * sin_signed


def _inproj_body(x_ref, gain_ref, w_ref, cos_ref, sin_ref, qg_ref, kg_ref, wgate_ref, bias_ref,
                 ones_ref, qa_ref, ka_ref, va_ref, qin_ref, kin_ref, vb_ref, rb_ref,
                 ga_ref, gb_ref, fac_ref, *raw_refs):
    h = _rms(x_ref[...], gain_ref[...]).astype(BF16)
    half = D_MODEL // 2
    lane = lax.broadcasted_iota(jnp.int32, (1, LANES), 1)

    def proj(which, lo=0, width=None):
        off = IN_OFF[which] + lo
        width = IN_SPLITS[which] if width is None else width
        return _dot_nt(h, w_ref[off:off + width, :])

    narrow = proj(P_QA, 0, IN_OFF[P_QB])
    tail = proj(P_RB, half, half + LANES)
    glr = jnp.where(lane < GLA_GATE_RANK, tail[:, half:], 0.0).astype(BF16)
    logits = _dot(glr, wgate_ref[...]) + bias_ref[...]
    r = tail[:, :half]
    rb_ref[:, half:] = (r * _sigmoid(r)).astype(BF16)
    q = proj(P_QB) * GLA_DK ** -0.5
    la = (jnp.minimum(logits, 0.0) - jnp.log1p(jnp.exp(-jnp.abs(logits)))) * (1.0 / GLA_TAU)
    la_hi, la_lo = _hi_lo(la)
    k = proj(P_KB)
    if raw_refs:
        raw_refs[0][...] = q.astype(BF16)
        raw_refs[1][...] = k.astype(BF16)

    c = GLA_CHUNK
    t_i = lax.broadcasted_iota(jnp.int32, (c, c), 0)
    s_i = lax.broadcasted_iota(jnp.int32, (c, c), 1)
    tril = (s_i <= t_i).astype(BF16)
    cums = []
    for ci in range(CHUNKS_PER_TILE):
        rows = slice(ci * c, (ci + 1) * c)
        cums.append(_dot(tril, la_hi[rows]) + _dot(tril, la_lo[rows]))

    def decay_products(ci):
        rows = slice(ci * c, (ci + 1) * c)
        b = cums[ci]
        b_last = b[c - 1:c, :]
        b_mid = b[c // 2 - 1:c // 2, :]
        qin_ref[rows, :] = (q[rows] * jnp.exp(b - b_mid)).astype(BF16)
        kin_ref[rows, :] = (k[rows] * jnp.exp(b_mid - b)).astype(BF16)
        fac_ref[0, ci:ci + 1, :] = jnp.exp(b_last)
        fac_ref[0, CHUNKS_PER_TILE + ci:CHUNKS_PER_TILE + ci + 1, :] = jnp.exp(b_mid)
        fac_ref[0, 2 * CHUNKS_PER_TILE + ci:2 * CHUNKS_PER_TILE + ci + 1, :] = jnp.exp(b_last - b_mid)
        if raw_refs:
            raw_refs[2][rows, :] = b

    cos, sin = cos_ref[...], sin_ref[...]
    qa = narrow[:, :SWA_Q_W]
    ssq_q = _dot((qa * qa).astype(BF16), ones_ref[...])

    def rope_q(col):
        cs = slice(col * LANES, (col + 1) * LANES)
        qc = _head_norm_rope(qa[:, cs], ssq_q[:, cs], qg_ref[...], cos, sin, lane)
        qa_ref[:, cs] = (qc * SWA_HEAD_DIM ** -0.5).astype(BF16)

    vb_ref[:, :half] = proj(P_VB, 0, half).astype(BF16)
    decay_products(0)
    r = proj(P_RB, 0, half)
    decay_products(1)
    rb_ref[:, :half] = (r * _sigmoid(r)).astype(BF16)
    decay_products(2)
    for col, (which, ref, lo) in enumerate(((P_GA, ga_ref, 0), (P_GA, ga_ref, half),
                                             (P_GB, gb_ref, 0), (P_GB, gb_ref, half))):
        ref[:, lo:lo + half] = _sigmoid(proj(which, lo, half)).astype(BF16)
        if col == 0:
            decay_products(3)
        rope_q(col)
    ka = narrow[:, IN_OFF[P_KA]:IN_OFF[P_VA]]
    ssq_k = _dot((ka * ka).astype(BF16), ones_ref[0:SWA_KV_W, 0:SWA_KV_W])
    ka_ref[...] = _head_norm_rope(ka, ssq_k, kg_ref[...], cos, sin, lane).astype(BF16)
    va_ref[...] = narrow[:, IN_OFF[P_VA]:IN_OFF[P_QB]].astype(BF16)
    vb_ref[:, half:] = proj(P_VB, half, half).astype(BF16)


def _inproj(x, gain, w, cos, sin, q_gain, k_gain, wgate, bias, head_ones, seq, raw=False):
    n = x.shape[0]
    nt = n // ROW_TILE
    tiles_per_seq = seq // ROW_TILE
    row = lambda i: (i, 0)
    pos = lambda i: (i % tiles_per_seq, 0)
    widths = (SWA_Q_W, SWA_KV_W, SWA_KV_W, GLA_K_W, GLA_K_W, GLA_V_W, GLA_V_W, D_MODEL, D_MODEL)
    raw_types = (BF16, BF16, F32) if raw else ()
    outs, _ = _dense_call(
        _inproj_body, "inproj", n,
        [pl.BlockSpec((ROW_TILE, D_MODEL), row), _whole((1, D_MODEL)), _whole((IN_COLS, D_MODEL)),
         pl.BlockSpec((ROW_TILE, LANES), pos), pl.BlockSpec((ROW_TILE, LANES), pos),
         _whole((1, LANES)), _whole((1, LANES)), _whole((GLR_PAD, GLA_K_W)), _whole((1, GLA_K_W)),
         _whole((SWA_Q_W, SWA_Q_W))],
        (x, gain, w, cos, sin, q_gain, k_gain, wgate, bias, head_ones),
        [pl.BlockSpec((ROW_TILE, width), row) for width in widths]
        + [pl.BlockSpec((1, GLA_FACTOR_ROWS, GLA_K_W), lambda i: (i, 0, 0))]
        + [pl.BlockSpec((ROW_TILE, GLA_K_W), row) for _ in raw_types],
        [jax.ShapeDtypeStruct((n, width), BF16) for width in widths]
        + [jax.ShapeDtypeStruct((nt, GLA_FACTOR_ROWS, GLA_K_W), F32)]
        + [jax.ShapeDtypeStruct((n, GLA_K_W), dtype) for dtype in raw_types])
    return outs


def _swa_setup(i, k_ref, v_ref, kbuf, vbuf):
    lane = lax.broadcasted_iota(jnp.int32, (1, LANES), 1)
    first_half = lane < SWA_HEAD_DIM

    @pl.when(i == 0)
    def _():
        kbuf[0:SWA_BLOCK, :] = jnp.zeros((SWA_BLOCK, LANES), BF16)
        vbuf[0:SWA_BLOCK, :] = jnp.zeros((SWA_BLOCK, LANES), BF16)

    kbuf[SWA_BLOCK:, :] = k_ref[...]
    vbuf[SWA_BLOCK:, :] = v_ref[...]
    kall = kbuf[...]
    vall = vbuf[...]
    kswap = pltpu.roll(kall, SWA_HEAD_DIM, 1)
    vswap = pltpu.roll(vall, SWA_HEAD_DIM, 1)
    kdup = [jnp.where(first_half, kall, kswap), jnp.where(first_half, kswap, kall)]
    vdup = [jnp.where(first_half, vall, vswap), jnp.where(first_half, vswap, vall)]

    rows = SWA_GROUP * SWA_BLOCK
    t_idx = lax.broadcasted_iota(jnp.int32, (rows, SWA_BLOCK), 0) % SWA_BLOCK
    c_idx = lax.broadcasted_iota(jnp.int32, (rows, SWA_BLOCK), 1)
    use_cur = c_idx <= t_idx
    row_head = lax.broadcasted_iota(jnp.int32, (rows, 1), 0) // SWA_BLOCK
    return first_half, kdup, vdup, use_cur, row_head


def _swa_scores(ctx, q_ref, kvh, blk):
    first_half, kdup = ctx[0], ctx[1]
    r0 = blk * SWA_BLOCK
    parts = []
    for g in range(SWA_GROUP):
        head = kvh * SWA_GROUP + g
        qc = q_ref[r0:r0 + SWA_BLOCK, (head // 2) * LANES:(head // 2 + 1) * LANES]
        keep = first_half if head % 2 == 0 else jnp.logical_not(first_half)
        parts.append(jnp.where(keep, qc, jnp.zeros_like(qc)))
    return _dot_nt(jnp.concatenate(parts, axis=0), kdup[kvh][r0:r0 + 2 * SWA_BLOCK, :])


def _swa_attend(ctx, i, sinks_ref, o_ref, s2, kvh, blk):
    first_half, _, vdup, use_cur, row_head = ctx
    rows = SWA_GROUP * SWA_BLOCK
    sink = jnp.zeros((rows, 1), F32)
    for g in range(SWA_GROUP):
        sink = jnp.where(row_head == g, sinks_ref[kvh * SWA_GROUP + g], sink)
    r0 = blk * SWA_BLOCK
    s_prev = s2[:, :SWA_BLOCK]
    if blk == 0:
        s_prev = s_prev + jnp.where(i > 0, 0.0, -jnp.inf)
    s = jnp.where(use_cur, s2[:, SWA_BLOCK:], s_prev)
    m = jnp.maximum(jnp.max(s, axis=-1, keepdims=True), sink)
    p = jnp.exp(s - m)
    denom = jnp.sum(p, axis=-1, keepdims=True) + jnp.exp(sink - m)
    p2 = jnp.concatenate([jnp.where(use_cur, 0.0, p), jnp.where(use_cur, p, 0.0)], axis=1)
    vals = vdup[kvh][r0:r0 + 2 * SWA_BLOCK, :]
    o = _dot(p2.astype(BF16), vals) / denom
    for pair in range(SWA_GROUP // 2):
        a = o[(2 * pair) * SWA_BLOCK:(2 * pair + 1) * SWA_BLOCK, :]
        b = o[(2 * pair + 1) * SWA_BLOCK:(2 * pair + 2) * SWA_BLOCK, :]
        col = (kvh * SWA_GROUP) // 2 + pair
        o_ref[r0:r0 + SWA_BLOCK, col * LANES:(col + 1) * LANES] = (
            jnp.where(first_half, a, b).astype(o_ref.dtype))


def _swa_carry(kbuf, vbuf):
    kbuf[0:SWA_BLOCK, :] = kbuf[ROW_TILE:ROW_TILE + SWA_BLOCK, :]
    vbuf[0:SWA_BLOCK, :] = vbuf[ROW_TILE:ROW_TILE + SWA_BLOCK, :]


def _decay_columns(decay_row):
    cols = jnp.transpose(jnp.broadcast_to(decay_row, (GLA_CHUNK, GLA_DK)))
    return jnp.concatenate([cols] * (GLA_DV // GLA_CHUNK), axis=1)


def _gla_tile(i, qin_ref, kin_ref, v_ref, fac_ref, o_ref, s_ref):
    c = GLA_CHUNK

    @pl.when(i == 0)
    def _():
        s_ref[...] = jnp.zeros(s_ref.shape, F32)

    t_i = lax.broadcasted_iota(jnp.int32, (c, c), 0)
    s_i = lax.broadcasted_iota(jnp.int32, (c, c), 1)
    causal = s_i <= t_i
    items = [(ci, h) for ci in range(CHUNKS_PER_TILE) for h in range(GLA_HEADS)]
    window = lambda ci, h: (slice(ci * c, (ci + 1) * c), slice(h * GLA_DK, (h + 1) * GLA_DK),
                            slice(h * GLA_DV, (h + 1) * GLA_DV))
    factor = lambda kind, ci, ks: fac_ref[0, kind * CHUNKS_PER_TILE + ci:kind * CHUNKS_PER_TILE + ci + 1, ks]

    attn, update = {}, {}
    for ci, h in items:
        rows, ks, vs = window(ci, h)
        scores = _dot_nt(qin_ref[rows, ks], kin_ref[rows, ks])
        attn[ci, h] = jnp.where(causal, scores, 0.0).astype(BF16)
        k_state = (kin_ref[rows, ks].astype(F32) * factor(2, ci, ks)).astype(BF16)
        update[ci, h] = _dot_tn(k_state, v_ref[rows, vs])

    states = [s_ref[h] for h in range(GLA_HEADS)]
    for ci, h in items:
        rows, ks, vs = window(ci, h)
        q_decayed = (qin_ref[rows, ks].astype(F32) * factor(1, ci, ks)).astype(BF16)
        lhs = jnp.concatenate([q_decayed, attn[ci, h]], axis=1)
        rhs = jnp.concatenate([states[h].astype(BF16), v_ref[rows, vs]], axis=0)
        o_ref[rows, vs] = _dot(lhs, rhs).astype(o_ref.dtype)
        states[h] = _decay_columns(factor(0, ci, ks)) * states[h] + update[ci, h]
    for h in range(GLA_HEADS):
        s_ref[h] = states[h]


def _swa_tile(i, sinks_ref, q_ref, k_ref, v_ref, o_ref, kbuf, vbuf):
    ctx = _swa_setup(i, k_ref, v_ref, kbuf, vbuf)
    items = [(kvh, blk) for kvh in range(SWA_KV_HEADS) for blk in range(ROW_TILE // SWA_BLOCK)]
    scores = [_swa_scores(ctx, q_ref, kvh, blk) for kvh, blk in items]
    for s2, (kvh, blk) in zip(scores, items):
        _swa_attend(ctx, i, sinks_ref, o_ref, s2, kvh, blk)
    _swa_carry(kbuf, vbuf)


def _mixer_body(sinks_ref, qa_ref, ka_ref, va_ref, qin_ref, kin_ref, vb_ref, fac_ref,
                x_ref, rb_ref, ga_ref, gb_ref, og_ref, wa_ref, wb_ref, wo_ref,
                o_ref, kbuf, vbuf, s_ref, oa_scr, ob_scr):
    i = pl.program_id(1)
    _gla_tile(i, qin_ref, kin_ref, vb_ref, fac_ref, ob_scr, s_ref)
    _swa_tile(i, sinks_ref, qa_ref, ka_ref, va_ref, oa_scr, kbuf, vbuf)
    _outproj_body(x_ref, oa_scr, ob_scr, rb_ref, ga_ref, gb_ref, og_ref, wa_ref, wb_ref, wo_ref, o_ref)


def _swa_scratch():
    return [pltpu.VMEM((ROW_TILE + SWA_BLOCK, LANES), BF16), pltpu.VMEM((ROW_TILE + SWA_BLOCK, LANES), BF16)]


def _mixer(qa, ka, va, sinks, qin, kin, vb, fac, x, rb, ga, gb, out_gain, wa, wb, wo, batch, seq):
    nt = seq // ROW_TILE
    tok = lambda b, i, *_: (b * nt + i, 0)
    spec = lambda width: pl.BlockSpec((ROW_TILE, width), tok)
    fixed = lambda shape: _resident(shape, lambda b, i, *_: (0,) * len(shape))
    return pl.pallas_call(
        _mixer_body,
        grid_spec=pltpu.PrefetchScalarGridSpec(
            num_scalar_prefetch=1,
            grid=(batch, nt),
            in_specs=[spec(SWA_Q_W), spec(SWA_KV_W), spec(SWA_KV_W),
                      spec(GLA_K_W), spec(GLA_K_W), spec(GLA_V_W),
                      pl.BlockSpec((1, GLA_FACTOR_ROWS, GLA_K_W), lambda b, i, *_: (b * nt + i, 0, 0)),
                      spec(D_MODEL), spec(GLA_V_W), spec(D_MODEL), spec(D_MODEL),
                      fixed((1, GLA_V_W)), fixed((SWA_Q_W, D_MODEL)), fixed((GLA_V_W, D_MODEL)),
                      fixed((D_MODEL, D_MODEL))],
            out_specs=spec(D_MODEL),
            scratch_shapes=_swa_scratch() + [pltpu.VMEM((GLA_HEADS, GLA_DK, GLA_DV), F32),
                                             pltpu.VMEM((ROW_TILE, SWA_Q_W), BF16),
                                             pltpu.VMEM((ROW_TILE, GLA_V_W), BF16)],
        ),
        out_shape=jax.ShapeDtypeStruct((batch * seq, D_MODEL), F32),
        compiler_params=_params(2),
        name="mixer",
    )(sinks, qa, ka, va, qin, kin, vb, fac, x, rb, ga, gb, out_gain, wa, wb, wo)


def _swa_body(sinks_ref, q_ref, k_ref, v_ref, o_ref, kbuf, vbuf):
    _swa_tile(pl.program_id(1), sinks_ref, q_ref, k_ref, v_ref, o_ref, kbuf, vbuf)


def _swa(qa, ka, va, sinks, batch, seq):
    nt = seq // ROW_TILE
    tok = lambda b, i, *_: (b * nt + i, 0)
    return pl.pallas_call(
        _swa_body,
        grid_spec=pltpu.PrefetchScalarGridSpec(
            num_scalar_prefetch=1,
            grid=(batch, nt),
            in_specs=[
                pl.BlockSpec((ROW_TILE, SWA_Q_W), tok),
                pl.BlockSpec((ROW_TILE, SWA_KV_W), tok),
                pl.BlockSpec((ROW_TILE, SWA_KV_W), tok),
            ],
            out_specs=pl.BlockSpec((ROW_TILE, SWA_Q_W), tok),
            scratch_shapes=_swa_scratch(),
        ),
        out_shape=jax.ShapeDtypeStruct((batch * seq, SWA_Q_W), BF16),
        compiler_params=_params(2),
        name="swa",
    )(sinks, qa, ka, va)


def _gla_exact_body(q_ref, k_ref, v_ref, b_ref, o_ref, s_ref, kf_ref, bf_ref):
    i = pl.program_id(1)
    c = GLA_CHUNK

    @pl.when(i == 0)
    def _():
        s_ref[...] = jnp.zeros(s_ref.shape, F32)

    t_col = lax.broadcasted_iota(jnp.int32, (c, 1), 0)
    s_lane = lax.broadcasted_iota(jnp.int32, (1, c), 1)

    for h in range(GLA_HEADS):
        ks = slice(h * GLA_DK, (h + 1) * GLA_DK)
        vs = slice(h * GLA_DV, (h + 1) * GLA_DV)

        def chunk_step(ci, carry, h=h, ks=ks, vs=vs):
            rows = pl.ds(pl.multiple_of(ci * c, c), c)
            b = b_ref[rows, ks]
            q = q_ref[rows, ks].astype(F32)
            k = k_ref[rows, ks].astype(F32)
            v = v_ref[rows, vs]
            kf_ref[...] = k
            bf_ref[...] = b

            def key_column(s, attn):
                pair = jnp.exp(jnp.minimum(b - bf_ref[pl.ds(s, 1), :], 0.0))
                col = jnp.sum(q * pair * kf_ref[pl.ds(s, 1), :], axis=-1, keepdims=True)
                col = jnp.where(t_col >= s, col, 0.0)
                return attn + col * (s_lane == s).astype(F32)

            attn = lax.fori_loop(0, c, key_column, jnp.zeros((c, c), F32))
            b_last = b[c - 1:c, :]
            state = s_ref[h]
            o = _dot((q * jnp.exp(b)).astype(BF16), state.astype(BF16)) + _dot(attn.astype(BF16), v)
            o_ref[rows, vs] = o.astype(o_ref.dtype)
            s_ref[h] = (_decay_columns(jnp.exp(b_last)) * state
                        + _dot_tn((k * jnp.exp(b_last - b)).astype(BF16), v))
            return carry

        lax.fori_loop(0, CHUNKS_PER_TILE, chunk_step, 0)


def _gla_exact(q, k, vb, bcum, batch, seq):
    nt = seq // ROW_TILE
    tok = lambda b, i: (b * nt + i, 0)
    kspec = pl.BlockSpec((ROW_TILE, GLA_K_W), tok)
    vspec = pl.BlockSpec((ROW_TILE, GLA_V_W), tok)
    return pl.pallas_call(
        _gla_exact_body,
        grid=(batch, nt),
        in_specs=[kspec, kspec, vspec, kspec],
        out_specs=vspec,
        out_shape=jax.ShapeDtypeStruct((batch * seq, GLA_V_W), BF16),
        scratch_shapes=[pltpu.VMEM((GLA_HEADS, GLA_DK, GLA_DV), F32),
                        pltpu.VMEM((GLA_CHUNK, GLA_DK), F32), pltpu.VMEM((GLA_CHUNK, GLA_DK), F32)],
        compiler_params=_params(2),
        name="gla_exact",
    )(q, k, vb, bcum)


def _outproj_body(x_ref, oa_ref, ob_ref, rb_ref, ga_ref, gb_ref, og_ref, wa_ref, wb_ref, wo_ref, o_ref):
    quarter = D_MODEL // GLA_HEADS
    ya = []
    gated = []
    for h in range(GLA_HEADS):
        vs = slice(h * GLA_DV, (h + 1) * GLA_DV)
        ya.append(_dot(oa_ref[...], wa_ref[:, h * quarter:(h + 1) * quarter]))
        on = _rms(ob_ref[:, vs].astype(F32), og_ref[:, vs])
        gated.append((on * rb_ref[:, vs].astype(F32)).astype(BF16))
    gated = jnp.concatenate(gated, axis=1)
    merged = []
    for j in range(GLA_HEADS):
        cs = slice(j * quarter, (j + 1) * quarter)
        yb = _dot(gated, wb_ref[:, cs])
        merged.append((ga_ref[:, cs].astype(F32) * ya[j] + gb_ref[:, cs].astype(F32) * yb).astype(BF16))
    o_ref[...] = x_ref[...] + _dot(jnp.concatenate(merged, axis=1), wo_ref[...])


def _outproj(x, oa, ob, rb, ga, gb, out_gain, wa, wb, wo):
    n = x.shape[0]
    spec = lambda width: pl.BlockSpec((ROW_TILE, width), lambda i: (i, 0))
    (out,), _ = _dense_call(
        _outproj_body, "outproj", n,
        [spec(D_MODEL), spec(SWA_Q_W), spec(GLA_V_W), spec(GLA_V_W), spec(D_MODEL), spec(D_MODEL),
         _whole((1, GLA_V_W)), _whole((SWA_Q_W, D_MODEL)), _whole((GLA_V_W, D_MODEL)),
         _whole((D_MODEL, D_MODEL))],
        (x, oa, ob, rb, ga, gb, out_gain, wa, wb, wo),
        [spec(D_MODEL)], [jax.ShapeDtypeStruct((n, D_MODEL), F32)])
    return out


def _rope_tables(seq):
    inv_freq = ROPE_THETA ** (-jnp.arange(0, SWA_HEAD_DIM, 2, dtype=F32) / SWA_HEAD_DIM)
    ang = jnp.arange(seq, dtype=F32)[:, None] * inv_freq[None, :]
    cos, sin = jnp.cos(ang), jnp.sin(ang)
    reps = LANES // SWA_HEAD_DIM
    cos_t = jnp.tile(jnp.concatenate([cos, cos], axis=-1), (1, reps))
    sin_t = jnp.tile(jnp.concatenate([-sin, sin], axis=-1), (1, reps))
    return cos_t, sin_t


def kernel(x, ffn1_norm, ffn1_w_gate, ffn1_w_up, ffn1_w_down, mix_norm, w_in, swa_q_norm, swa_k_norm, swa_sinks, gla_w_gate, gla_gate_bias, gla_out_norm, w_proj_a, w_proj_b, w_out, ffn2_norm, ffn2_w_gate, ffn2_w_up, ffn2_w_down):
    batch, seq, d = x.shape
    assert d == D_MODEL and seq % ROW_TILE == 0 and (batch * seq) % FFN_TILE == 0
    n = batch * seq
    cos_t, sin_t = _rope_tables(seq)

    bf = lambda w: w.astype(BF16)
    wgate_p = jnp.concatenate(
        [gla_w_gate, jnp.zeros((DEPTH, GLR_PAD - GLA_GATE_RANK, GLA_K_W), F32)], axis=1).astype(BF16)
    head_id = jnp.arange(SWA_Q_W, dtype=jnp.int32) // SWA_HEAD_DIM
    head_ones = (head_id[:, None] == head_id[None, :]).astype(BF16)
    reps = LANES // SWA_HEAD_DIM
    q_gain = jnp.tile(swa_q_norm, (1, reps))
    k_gain = jnp.tile(swa_k_norm, (1, reps))
    row = lambda g, l: g[l:l + 1]

    w_in_t = jnp.transpose(w_in, (0, 2, 1))
    ffn1_w = (ffn1_w_gate, ffn1_w_up, ffn1_w_down)
    ffn2_w = (ffn2_w_gate, ffn2_w_up, ffn2_w_down)
    mix_w = (w_proj_a, w_proj_b, w_out)
    at = lambda ws, l: [(w, l) for w in ws]

    f1 = tuple(bf(w[0]) for w in ffn1_w)
    h = x.reshape(n, D_MODEL)
    for l in range(DEPTH):
        convs = [(w_in_t, l)] + at(ffn2_w, l) + (at(mix_w, 0) if l == 0 else [])
        h, conv = _ffn(h, row(ffn1_norm, l), *f1, convs=convs)
        w_t, f2 = conv[0], conv[1:4]
        if l == 0:
            wa, wb, wo = conv[4:]
        proj_args = (h, row(mix_norm, l), w_t, cos_t, sin_t, row(q_gain, l), row(k_gain, l), wgate_p[l],
                     row(gla_gate_bias, l), head_ones)
        qa, ka, va, qin, kin, vb, rb, ga, gb, fac = _inproj(*proj_args, seq)
        tail = (h, rb, ga, gb, row(gla_out_norm, l), wa, wb, wo)

        def exact_path(*ops):
            qraw, kraw, bcum = _inproj(*ops[16:], seq, raw=True)[-3:]
            ob = _gla_exact(qraw, kraw, ops[6], bcum, batch, seq)
            return _outproj(ops[8], _swa(*ops[:4], batch, seq), ob, *ops[9:16])

        h = lax.cond(
            jnp.min(fac[:, :CHUNKS_PER_TILE]) >= GLA_MIN_CHUNK_DECAY,
            lambda *ops: _mixer(*ops[:16], batch, seq), exact_path,
            qa, ka, va, swa_sinks[l], qin, kin, vb, fac, *tail, *proj_args)
        convs = [] if l + 1 == DEPTH else at(ffn1_w, l + 1) + at(mix_w, l + 1)
        h, conv = _ffn(h, row(ffn2_norm, l), *f2, convs=convs)
        if convs:
            f1, (wa, wb, wo) = conv[:3], conv[3:]
    return h.reshape(batch, seq, D_MODEL)
```

```python
import functools

import jax
import jax.numpy as jnp
from jax import lax
from jax.experimental import pallas as pl
from jax.experimental.pallas import tpu as pltpu

F32 = jnp.float32
BF16 = jnp.bfloat16

D_MODEL = 1024
DEPTH = 4
D_FF = 2816
EPS = 1e-6

SWA_HEADS = 8
SWA_KV_HEADS = 2
SWA_GROUP = SWA_HEADS // SWA_KV_HEADS
SWA_HEAD_DIM = 64
SWA_BLOCK = 128
ROPE_THETA = 10000.0
SWA_Q_W = SWA_HEADS * SWA_HEAD_DIM
SWA_KV_W = SWA_KV_HEADS * SWA_HEAD_DIM

GLA_HEADS = 4
GLA_DK = 128
GLA_DV = 256
GLA_GATE_RANK = 16
GLA_TAU = 16.0
GLA_K_W = GLA_HEADS * GLA_DK
GLA_V_W = GLA_HEADS * GLA_DV

IN_SPLITS = (SWA_Q_W, SWA_KV_W, SWA_KV_W, GLA_K_W, GLA_K_W, GLA_V_W, GLA_V_W, GLA_GATE_RANK, D_MODEL, D_MODEL)

LANES = 128
GLR_PAD = LANES
IN_COLS = sum(IN_SPLITS)
IN_OFF = tuple(sum(IN_SPLITS[:j]) for j in range(len(IN_SPLITS)))
(P_QA, P_KA, P_VA, P_QB, P_KB, P_VB, P_RB, P_GLR, P_GA, P_GB) = range(len(IN_SPLITS))

ROW_TILE = 512
FFN_TILE = 1024
INPROJ_TILE = 1024
FFN_SUB = 256
GLA_CHUNK = 128
CHUNKS_PER_TILE = ROW_TILE // GLA_CHUNK
GLA_FACTOR_ROWS = 3 * CHUNKS_PER_TILE
VMEM_LIMIT = 56 * 1024 * 1024
GLA_MIN_CHUNK_DECAY = 8.7e-27


def _rms(x, gain):
    ms = jnp.mean(x * x, axis=-1, keepdims=True)
    return x * lax.rsqrt(ms + EPS) * gain


def _sigmoid(x):
    return 0.5 * jnp.tanh(0.5 * x) + 0.5


def _dot(a, b):
    return jnp.dot(a, b, preferred_element_type=F32)


def _dot_nt(a, b):
    return lax.dot_general(a, b, (((1,), (1,)), ((), ())), preferred_element_type=F32)


def _dot_tn(a, b):
    return lax.dot_general(a, b, (((0,), (0,)), ((), ())), preferred_element_type=F32)


def _hi_lo(x):
    hi = x.astype(BF16)
    return hi, (x - hi.astype(F32)).astype(BF16)


def _resident(shape, index_map):
    return pl.BlockSpec(shape, index_map, pipeline_mode=pl.Buffered(1))


def _params(n_axes):
    return pltpu.CompilerParams(dimension_semantics=("arbitrary",) * n_axes,
                                vmem_limit_bytes=VMEM_LIMIT)


BF16_SUBLANES = 16


def _dense_call(body, name, n, in_specs, args, out_specs, out_shapes, convs=None, tile=None):
    nsteps = n // (tile or ROW_TILE)
    n_in, n_out, n_conv = len(in_specs), len(out_specs), len(convs or ())
    in_specs, out_specs, out_shapes, args = list(in_specs), list(out_specs), list(out_shapes), list(args)
    for w, layer in convs or ():
        _, rows, cols = w.shape
        nblk = max(b for b in range(1, nsteps + 1) if rows % (b * BF16_SUBLANES) == 0)
        last = nblk - 1
        in_specs.append(pl.BlockSpec((None, rows // nblk, cols),
                                     lambda i, layer=layer, last=last: (layer, jnp.minimum(i, last), 0)))
        out_specs.append(pl.BlockSpec((rows // nblk, cols), lambda i, last=last: (jnp.minimum(i, last), 0)))
        out_shapes.append(jax.ShapeDtypeStruct((rows, cols), BF16))
        args.append(w)

    def wrapped(*refs):
        ins, srcs = refs[:n_in], refs[n_in:n_in + n_conv]
        outs, dsts = refs[n_in + n_conv:n_in + n_conv + n_out], refs[n_in + n_conv + n_out:]

        def cast_weights():
            for s, d in zip(srcs, dsts):
                d[...] = s[...].astype(BF16)

        if convs is None:
            body(*ins, *outs)
        else:
            body(cast_weights, *ins, *outs)

    res = pl.pallas_call(
        wrapped, grid=(nsteps,), in_specs=in_specs, out_specs=out_specs, out_shape=out_shapes,
        compiler_params=_params(1), name=name)(*args)
    return res[:n_out], res[n_out:]


def _whole(shape):
    return _resident(shape, lambda i: (0,) * len(shape))


def _ffn_body(cast_weights, x_ref, gain_ref, wg_ref, wu_ref, wd_ref, o_ref):
    nsub = FFN_TILE // FFN_SUB
    rows = [slice(j * FFN_SUB, (j + 1) * FFN_SUB) for j in range(nsub)]

    def up(j):
        h = _rms(x_ref[rows[j], :], gain_ref[...]).astype(BF16)
        return _dot(h, wg_ref[...]), _dot(h, wu_ref[...])

    def down(j, g, u):
        a = (g * _sigmoid(g) * u).astype(BF16)
        o_ref[rows[j], :] = x_ref[rows[j], :] + 0.5 * _dot(a, wd_ref[...])

    pending = up(0)
    for j in range(nsub):
        nxt = up(j + 1) if j + 1 < nsub else None
        if j == 0:
            cast_weights()
        down(j, *pending)
        pending = nxt


def _ffn(x, gain, wg, wu, wd, convs=()):
    n = x.shape[0]
    row = pl.BlockSpec((FFN_TILE, D_MODEL), lambda i: (i, 0))
    (out,), converted = _dense_call(
        _ffn_body, "ffn", n,
        [row, _whole((1, D_MODEL)), _whole((D_MODEL, D_FF)), _whole((D_MODEL, D_FF)), _whole((D_FF, D_MODEL))],
        (x, gain, wg, wu, wd), [row], [jax.ShapeDtypeStruct((n, D_MODEL), F32)], convs, tile=FFN_TILE)
    return out, converted


def _head_norm_rope(x, ssq, gain, cos, sin_signed, lane):
    xn = x * lax.rsqrt(ssq * (1.0 / SWA_HEAD_DIM) + EPS) * gain
    half = SWA_HEAD_DIM // 2
    partner = jnp.where((lane % SWA_HEAD_DIM) < half,
                        pltpu.roll(xn, LANES - half, 1), pltpu.roll(xn, half, 1))
    return xn * cos + partner * sin_signed


def _inproj_body(x_ref, gain_ref, w_ref, cos_ref, sin_ref, qg_ref, kg_ref, wgate_ref, bias_ref,
                 ones_ref, qa_ref, ka_ref, va_ref, qin_ref, kin_ref, vb_ref, rb_ref,
                 ga_ref, gb_ref, fac_ref, *raw_refs):
    h = _rms(x_ref[...], gain_ref[...]).astype(BF16)
    half = D_MODEL // 2
    lane = lax.broadcasted_iota(jnp.int32, (1, LANES), 1)

    def proj(which, lo=0, width=None):
        off = IN_OFF[which] + lo
        width = IN_SPLITS[which] if width is None else width
        return _dot_nt(h, w_ref[off:off + width, :])

    narrow = proj(P_QA, 0, IN_OFF[P_QB])
    tail = proj(P_RB, half, half + LANES)
    glr = jnp.where(lane < GLA_GATE_RANK, tail[:, half:], 0.0).astype(BF16)
    logits = _dot(glr, wgate_ref[...]) + bias_ref[...]
    r = tail[:, :half]
    rb_ref[:, half:] = (r * _sigmoid(r)).astype(BF16)
    q = proj(P_QB) * GLA_DK ** -0.5
    la = (jnp.minimum(logits, 0.0) - jnp.log1p(jnp.exp(-jnp.abs(logits)))) * (1.0 / GLA_TAU)
    la_hi, la_lo = _hi_lo(la)
    k = proj(P_KB)
    if raw_refs:
        raw_refs[0][...] = q.astype(BF16)
        raw_refs[1][...] = k.astype(BF16)

    c = GLA_CHUNK
    t_i = lax.broadcasted_iota(jnp.int32, (c, c), 0)
    s_i = lax.broadcasted_iota(jnp.int32, (c, c), 1)
    tril = (s_i <= t_i).astype(BF16)
    cums = []
    for ci in range(CHUNKS_PER_TILE):
        rows = slice(ci * c, (ci + 1) * c)
        cums.append(_dot(tril, la_hi[rows]) + _dot(tril, la_lo[rows]))

    def decay_products(ci):
        rows = slice(ci * c, (ci + 1) * c)
        b = cums[ci]
        b_last = b[c - 1:c, :]
        b_mid = b[c // 2 - 1:c // 2, :]
        qin_ref[rows, :] = (q[rows] * jnp.exp(b - b_mid)).astype(BF16)
        kin_ref[rows, :] = (k[rows] * jnp.exp(b_mid - b)).astype(BF16)
        fac_ref[0, ci:ci + 1, :] = jnp.exp(b_last)
        fac_ref[0, CHUNKS_PER_TILE + ci:CHUNKS_PER_TILE + ci + 1, :] = jnp.exp(b_mid)
        fac_ref[0, 2 * CHUNKS_PER_TILE + ci:2 * CHUNKS_PER_TILE + ci + 1, :] = jnp.exp(b_last - b_mid)
        if raw_refs:
            raw_refs[2][rows, :] = b

    cos, sin = cos_ref[...], sin_ref[...]
    qa = narrow[:, :SWA_Q_W]
    ssq_q = _dot((qa * qa).astype(BF16), ones_ref[...])

    def rope_q(col):
        cs = slice(col * LANES, (col + 1) * LANES)
        qc = _head_norm_rope(qa[:, cs], ssq_q[:, cs], qg_ref[...], cos, sin, lane)
        qa_ref[:, cs] = (qc * SWA_HEAD_DIM ** -0.5).astype(BF16)

    vb_ref[:, :half] = proj(P_VB, 0, half).astype(BF16)
    decay_products(0)
    r = proj(P_RB, 0, half)
    decay_products(1)
    rb_ref[:, :half] = (r * _sigmoid(r)).astype(BF16)
    decay_products(2)
    for col, (which, ref, lo) in enumerate(((P_GA, ga_ref, 0), (P_GA, ga_ref, half),
                                             (P_GB, gb_ref, 0), (P_GB, gb_ref, half))):
        ref[:, lo:lo + half] = _sigmoid(proj(which, lo, half)).astype(BF16)
        if col == 0:
            decay_products(3)
        rope_q(col)
    ka = narrow[:, IN_OFF[P_KA]:IN_OFF[P_VA]]
    ssq_k = _dot((ka * ka).astype(BF16), ones_ref[0:SWA_KV_W, 0:SWA_KV_W])
    ka_ref[...] = _head_norm_rope(ka, ssq_k, kg_ref[...], cos, sin, lane).astype(BF16)
    va_ref[...] = narrow[:, IN_OFF[P_VA]:IN_OFF[P_QB]].astype(BF16)
    vb_ref[:, half:] = proj(P_VB, half, half).astype(BF16)


N_PROJ_INPUTS = 10
PROJ_ROW_TILED = (0, 3, 4)


def _inproj_step(subs, *refs):
    fac_index = N_PROJ_INPUTS + 9
    for sub in range(subs):
        rows = pl.ds(sub * ROW_TILE, ROW_TILE)
        views = []
        for j, ref in enumerate(refs):
            if j == fac_index:
                views.append(ref.at[pl.ds(sub, 1)])
            elif j >= N_PROJ_INPUTS or j in PROJ_ROW_TILED:
                views.append(ref.at[rows])
            else:
                views.append(ref)
        _inproj_body(*views)


def _inproj(x, gain, w, cos, sin, q_gain, k_gain, wgate, bias, head_ones, seq, raw=False):
    n = x.shape[0]
    nt = n // ROW_TILE
    tile = ROW_TILE if raw else INPROJ_TILE
    subs = tile // ROW_TILE
    steps_per_seq = seq // tile
    row = lambda i: (i, 0)
    pos = lambda i: (i % steps_per_seq, 0)
    widths = (SWA_Q_W, SWA_KV_W, SWA_KV_W, GLA_K_W, GLA_K_W, GLA_V_W, GLA_V_W, D_MODEL, D_MODEL)
    raw_types = (BF16, BF16, F32) if raw else ()
    outs, _ = _dense_call(
        functools.partial(_inproj_step, subs), "inproj", n,
        [pl.BlockSpec((tile, D_MODEL), row), _whole((1, D_MODEL)), _whole((IN_COLS, D_MODEL)),
         pl.BlockSpec((tile, LANES), pos), pl.BlockSpec((tile, LANES), pos),
         _whole((1, LANES)), _whole((1, LANES)), _whole((GLR_PAD, GLA_K_W)), _whole((1, GLA_K_W)),
         _whole((SWA_Q_W, SWA_Q_W))],
        (x, gain, w, cos, sin, q_gain, k_gain, wgate, bias, head_ones),
        [pl.BlockSpec((tile, width), row) for width in widths]
        + [pl.BlockSpec((subs, GLA_FACTOR_ROWS, GLA_K_W), lambda i: (i, 0, 0))]
        + [pl.BlockSpec((tile, GLA_K_W), row) for _ in raw_types],
        [jax.ShapeDtypeStruct((n, width), BF16) for width in widths]
        + [jax.ShapeDtypeStruct((nt, GLA_FACTOR_ROWS, GLA_K_W), F32)]
        + [jax.ShapeDtypeStruct((n, GLA_K_W), dtype) for dtype in raw_types],
        tile=tile)
    return outs


def _swa_setup(i, k_ref, v_ref, kbuf, vbuf):
    lane = lax.broadcasted_iota(jnp.int32, (1, LANES), 1)
    first_half = lane < SWA_HEAD_DIM

    @pl.when(i == 0)
    def _():
        kbuf[0:SWA_BLOCK, :] = jnp.zeros((SWA_BLOCK, LANES), BF16)
        vbuf[0:SWA_BLOCK, :] = jnp.zeros((SWA_BLOCK, LANES), BF16)

    kbuf[SWA_BLOCK:, :] = k_ref[...]
    vbuf[SWA_BLOCK:, :] = v_ref[...]
    kall = kbuf[...]
    vall = vbuf[...]
    kswap = pltpu.roll(kall, SWA_HEAD_DIM, 1)
    vswap = pltpu.roll(vall, SWA_HEAD_DIM, 1)
    kdup = [jnp.where(first_half, kall, kswap), jnp.where(first_half, kswap, kall)]
    vdup = [jnp.where(first_half, vall, vswap), jnp.where(first_half, vswap, vall)]

    rows = SWA_GROUP * SWA_BLOCK
    t_idx = lax.broadcasted_iota(jnp.int32, (rows, SWA_BLOCK), 0) % SWA_BLOCK
    c_idx = lax.broadcasted_iota(jnp.int32, (rows, SWA_BLOCK), 1)
    use_cur = c_idx <= t_idx
    row_head = lax.broadcasted_iota(jnp.int32, (rows, 1), 0) // SWA_BLOCK
    return first_half, kdup, vdup, use_cur, row_head


def _swa_scores(ctx, q_ref, kvh, blk):
    first_half, kdup = ctx[0], ctx[1]
    r0 = blk * SWA_BLOCK
    parts = []
    for g in range(SWA_GROUP):
        head = kvh * SWA_GROUP + g
        qc = q_ref[r0:r0 + SWA_BLOCK, (head // 2) * LANES:(head // 2 + 1) * LANES]
        keep = first_half if head % 2 == 0 else jnp.logical_not(first_half)
        parts.append(jnp.where(keep, qc, jnp.zeros_like(qc)))
    return _dot_nt(jnp.concatenate(parts, axis=0), kdup[kvh][r0:r0 + 2 * SWA_BLOCK, :])


def _swa_attend(ctx, i, sinks_ref, o_ref, s2, kvh, blk):
    first_half, _, vdup, use_cur, row_head = ctx
    rows = SWA_GROUP * SWA_BLOCK
    sink = jnp.zeros((rows, 1), F32)
    for g in range(SWA_GROUP):
        sink = jnp.where(row_head == g, sinks_ref[kvh * SWA_GROUP + g], sink)
    r0 = blk * SWA_BLOCK
    s_prev = s2[:, :SWA_BLOCK]
    if blk == 0:
        s_prev = s_prev + jnp.where(i > 0, 0.0, -jnp.inf)
    s = jnp.where(use_cur, s2[:, SWA_BLOCK:], s_prev)
    m = jnp.maximum(jnp.max(s, axis=-1, keepdims=True), sink)
    p = jnp.exp(s - m)
    denom = jnp.sum(p, axis=-1, keepdims=True) + jnp.exp(sink - m)
    p2 = jnp.concatenate([jnp.where(use_cur, 0.0, p), jnp.where(use_cur, p, 0.0)], axis=1)
    vals = vdup[kvh][r0:r0 + 2 * SWA_BLOCK, :]
    o = _dot(p2.astype(BF16), vals) / denom
    for pair in range(SWA_GROUP // 2):
        a = o[(2 * pair) * SWA_BLOCK:(2 * pair + 1) * SWA_BLOCK, :]
        b = o[(2 * pair + 1) * SWA_BLOCK:(2 * pair + 2) * SWA_BLOCK, :]
        col = (kvh * SWA_GROUP) // 2 + pair
        o_ref[r0:r0 + SWA_BLOCK, col * LANES:(col + 1) * LANES] = (
            jnp.where(first_half, a, b).astype(o_ref.dtype))


def _swa_carry(kbuf, vbuf):
    kbuf[0:SWA_BLOCK, :] = kbuf[ROW_TILE:ROW_TILE + SWA_BLOCK, :]
    vbuf[0:SWA_BLOCK, :] = vbuf[ROW_TILE:ROW_TILE + SWA_BLOCK, :]


def _decay_columns(decay_row):
    cols = jnp.transpose(jnp.broadcast_to(decay_row, (GLA_CHUNK, GLA_DK)))
    return jnp.concatenate([cols] * (GLA_DV // GLA_CHUNK), axis=1)


def _gla_tile(i, qin_ref, kin_ref, v_ref, fac_ref, o_ref, s_ref):
    c = GLA_CHUNK

    @pl.when(i == 0)
    def _():
        s_ref[...] = jnp.zeros(s_ref.shape, F32)

    t_i = lax.broadcasted_iota(jnp.int32, (c, c), 0)
    s_i = lax.broadcasted_iota(jnp.int32, (c, c), 1)
    causal = s_i <= t_i
    items = [(ci, h) for ci in range(CHUNKS_PER_TILE) for h in range(GLA_HEADS)]
    window = lambda ci, h: (slice(ci * c, (ci + 1) * c), slice(h * GLA_DK, (h + 1) * GLA_DK),
                            slice(h * GLA_DV, (h + 1) * GLA_DV))
    factor = lambda kind, ci, ks: fac_ref[0, kind * CHUNKS_PER_TILE + ci:kind * CHUNKS_PER_TILE + ci + 1, ks]

    attn, update = {}, {}
    for ci, h in items:
        rows, ks, vs = window(ci, h)
        scores = _dot_nt(qin_ref[rows, ks], kin_ref[rows, ks])
        attn[ci, h] = jnp.where(causal, scores, 0.0).astype(BF16)
        k_state = (kin_ref[rows, ks].astype(F32) * factor(2, ci, ks)).astype(BF16)
        update[ci, h] = _dot_tn(k_state, v_ref[rows, vs])

    states = [s_ref[h] for h in range(GLA_HEADS)]
    for ci, h in items:
        rows, ks, vs = window(ci, h)
        q_decayed = (qin_ref[rows, ks].astype(F32) * factor(1, ci, ks)).astype(BF16)
        lhs = jnp.concatenate([q_decayed, attn[ci, h]], axis=1)
        rhs = jnp.concatenate([states[h].astype(BF16), v_ref[rows, vs]], axis=0)
        o_ref[rows, vs] = _dot(lhs, rhs).astype(o_ref.dtype)
        states[h] = _decay_columns(factor(0, ci, ks)) * states[h] + update[ci, h]
    for h in range(GLA_HEADS):
        s_ref[h] = states[h]


def _swa_tile(i, sinks_ref, q_ref, k_ref, v_ref, o_ref, kbuf, vbuf):
    ctx = _swa_setup(i, k_ref, v_ref, kbuf, vbuf)
    items = [(kvh, blk) for kvh in range(SWA_KV_HEADS) for blk in range(ROW_TILE // SWA_BLOCK)]
    scores = [_swa_scores(ctx, q_ref, kvh, blk) for kvh, blk in items]
    for s2, (kvh, blk) in zip(scores, items):
        _swa_attend(ctx, i, sinks_ref, o_ref, s2, kvh, blk)
    _swa_carry(kbuf, vbuf)


def _mixer_body(sinks_ref, qa_ref, ka_ref, va_ref, qin_ref, kin_ref, vb_ref, fac_ref,
                x_ref, rb_ref, ga_ref, gb_ref, og_ref, wa_ref, wb_ref, wo_ref,
                o_ref, kbuf, vbuf, s_ref, oa_scr, ob_scr):
    i = pl.program_id(1)
    _gla_tile(i, qin_ref, kin_ref, vb_ref, fac_ref, ob_scr, s_ref)
    _swa_tile(i, sinks_ref, qa_ref, ka_ref, va_ref, oa_scr, kbuf, vbuf)
    _outproj_body(x_ref, oa_scr, ob_scr, rb_ref, ga_ref, gb_ref, og_ref, wa_ref, wb_ref, wo_ref, o_ref)


def _swa_scratch():
    return [pltpu.VMEM((ROW_TILE + SWA_BLOCK, LANES), BF16), pltpu.VMEM((ROW_TILE + SWA_BLOCK, LANES), BF16)]


def _mixer(qa, ka, va, sinks, qin, kin, vb, fac, x, rb, ga, gb, out_gain, wa, wb, wo, batch, seq):
    nt = seq // ROW_TILE
    tok = lambda b, i, *_: (b * nt + i, 0)
    spec = lambda width: pl.BlockSpec((ROW_TILE, width), tok)
    fixed = lambda shape: _resident(shape, lambda b, i, *_: (0,) * len(shape))
    return pl.pallas_call(
        _mixer_body,
        grid_spec=pltpu.PrefetchScalarGridSpec(
            num_scalar_prefetch=1,
            grid=(batch, nt),
            in_specs=[spec(SWA_Q_W), spec(SWA_KV_W), spec(SWA_KV_W),
                      spec(GLA_K_W), spec(GLA_K_W), spec(GLA_V_W),
                      pl.BlockSpec((1, GLA_FACTOR_ROWS, GLA_K_W), lambda b, i, *_: (b * nt + i, 0, 0)),
                      spec(D_MODEL), spec(GLA_V_W), spec(D_MODEL), spec(D_MODEL),
                      fixed((1, GLA_V_W)), fixed((SWA_Q_W, D_MODEL)), fixed((GLA_V_W, D_MODEL)),
                      fixed((D_MODEL, D_MODEL))],
            out_specs=spec(D_MODEL),
            scratch_shapes=_swa_scratch() + [pltpu.VMEM((GLA_HEADS, GLA_DK, GLA_DV), F32),
                                             pltpu.VMEM((ROW_TILE, SWA_Q_W), BF16),
                                             pltpu.VMEM((ROW_TILE, GLA_V_W), BF16)],
        ),
        out_shape=jax.ShapeDtypeStruct((batch * seq, D_MODEL), F32),
        compiler_params=_params(2),
        name="mixer",
    )(sinks, qa, ka, va, qin, kin, vb, fac, x, rb, ga, gb, out_gain, wa, wb, wo)


def _swa_body(sinks_ref, q_ref, k_ref, v_ref, o_ref, kbuf, vbuf):
    _swa_tile(pl.program_id(1), sinks_ref, q_ref, k_ref, v_ref, o_ref, kbuf, vbuf)


def _swa(qa, ka, va, sinks, batch, seq):
    nt = seq // ROW_TILE
    tok = lambda b, i, *_: (b * nt + i, 0)
    return pl.pallas_call(
        _swa_body,
        grid_spec=pltpu.PrefetchScalarGridSpec(
            num_scalar_prefetch=1,
            grid=(batch, nt),
            in_specs=[
                pl.BlockSpec((ROW_TILE, SWA_Q_W), tok),
                pl.BlockSpec((ROW_TILE, SWA_KV_W), tok),
                pl.BlockSpec((ROW_TILE, SWA_KV_W), tok),
            ],
            out_specs=pl.BlockSpec((ROW_TILE, SWA_Q_W), tok),
            scratch_shapes=_swa_scratch(),
        ),
        out_shape=jax.ShapeDtypeStruct((batch * seq, SWA_Q_W), BF16),
        compiler_params=_params(2),
        name="swa",
    )(sinks, qa, ka, va)


def _gla_exact_body(q_ref, k_ref, v_ref, b_ref, o_ref, s_ref, kf_ref, bf_ref):
    i = pl.program_id(1)
    c = GLA_CHUNK

    @pl.when(i == 0)
    def _():
        s_ref[...] = jnp.zeros(s_ref.shape, F32)

    t_col = lax.broadcasted_iota(jnp.int32, (c, 1), 0)
    s_lane = lax.broadcasted_iota(jnp.int32, (1, c), 1)

    for h in range(GLA_HEADS):
        ks = slice(h * GLA_DK, (h + 1) * GLA_DK)
        vs = slice(h * GLA_DV, (h + 1) * GLA_DV)

        def chunk_step(ci, carry, h=h, ks=ks, vs=vs):
            rows = pl.ds(pl.multiple_of(ci * c, c), c)
            b = b_ref[rows, ks]
            q = q_ref[rows, ks].astype(F32)
            k = k_ref[rows, ks].astype(F32)
            v = v_ref[rows, vs]
            kf_ref[...] = k
            bf_ref[...] = b

            def key_column(s, attn):
                pair = jnp.exp(jnp.minimum(b - bf_ref[pl.ds(s, 1), :], 0.0))
                col = jnp.sum(q * pair * kf_ref[pl.ds(s, 1), :], axis=-1, keepdims=True)
                col = jnp.where(t_col >= s, col, 0.0)
                return attn + col * (s_lane == s).astype(F32)

            attn = lax.fori_loop(0, c, key_column, jnp.zeros((c, c), F32))
            b_last = b[c - 1:c, :]
            state = s_ref[h]
            o = _dot((q * jnp.exp(b)).astype(BF16), state.astype(BF16)) + _dot(attn.astype(BF16), v)
            o_ref[rows, vs] = o.astype(o_ref.dtype)
            s_ref[h] = (_decay_columns(jnp.exp(b_last)) * state
                        + _dot_tn((k * jnp.exp(b_last - b)).astype(BF16), v))
            return carry

        lax.fori_loop(0, CHUNKS_PER_TILE, chunk_step, 0)


def _gla_exact(q, k, vb, bcum, batch, seq):
    nt = seq // ROW_TILE
    tok = lambda b, i: (b * nt + i, 0)
    kspec = pl.BlockSpec((ROW_TILE, GLA_K_W), tok)
    vspec = pl.BlockSpec((ROW_TILE, GLA_V_W), tok)
    return pl.pallas_call(
        _gla_exact_body,
        grid=(batch, nt),
        in_specs=[kspec, kspec, vspec, kspec],
        out_specs=vspec,
        out_shape=jax.ShapeDtypeStruct((batch * seq, GLA_V_W), BF16),
        scratch_shapes=[pltpu.VMEM((GLA_HEADS, GLA_DK, GLA_DV), F32),
                        pltpu.VMEM((GLA_CHUNK, GLA_DK), F32), pltpu.VMEM((GLA_CHUNK, GLA_DK), F32)],
        compiler_params=_params(2),
        name="gla_exact",
    )(q, k, vb, bcum)


def _outproj_body(x_ref, oa_ref, ob_ref, rb_ref, ga_ref, gb_ref, og_ref, wa_ref, wb_ref, wo_ref, o_ref):
    quarter = D_MODEL // GLA_HEADS
    ya = []
    gated = []
    for h in range(GLA_HEADS):
        vs = slice(h * GLA_DV, (h + 1) * GLA_DV)
        ya.append(_dot(oa_ref[...], wa_ref[:, h * quarter:(h + 1) * quarter]))
        on = _rms(ob_ref[:, vs].astype(F32), og_ref[:, vs])
        gated.append((on * rb_ref[:, vs].astype(F32)).astype(BF16))
    gated = jnp.concatenate(gated, axis=1)
    merged = []
    for j in range(GLA_HEADS):
        cs = slice(j * quarter, (j + 1) * quarter)
        yb = _dot(gated, wb_ref[:, cs])
        merged.append((ga_ref[:, cs].astype(F32) * ya[j] + gb_ref[:, cs].astype(F32) * yb).astype(BF16))
    o_ref[...] = x_ref[...] + _dot(jnp.concatenate(merged, axis=1), wo_ref[...])


def _outproj(x, oa, ob, rb, ga, gb, out_gain, wa, wb, wo):
    n = x.shape[0]
    spec = lambda width: pl.BlockSpec((ROW_TILE, width), lambda i: (i, 0))
    (out,), _ = _dense_call(
        _outproj_body, "outproj", n,
        [spec(D_MODEL), spec(SWA_Q_W), spec(GLA_V_W), spec(GLA_V_W), spec(D_MODEL), spec(D_MODEL),
         _whole((1, GLA_V_W)), _whole((SWA_Q_W, D_MODEL)), _whole((GLA_V_W, D_MODEL)),
         _whole((D_MODEL, D_MODEL))],
        (x, oa, ob, rb, ga, gb, out_gain, wa, wb, wo),
        [spec(D_MODEL)], [jax.ShapeDtypeStruct((n, D_MODEL), F32)])
    return out


def _rope_tables(seq):
    inv_freq = ROPE_THETA ** (-jnp.arange(0, SWA_HEAD_DIM, 2, dtype=F32) / SWA_HEAD_DIM)
    ang = jnp.arange(seq, dtype=F32)[:, None] * inv_freq[None, :]
    cos, sin = jnp.cos(ang), jnp.sin(ang)
    reps = LANES // SWA_HEAD_DIM
    cos_t = jnp.tile(jnp.concatenate([cos, cos], axis=-1), (1, reps))
    sin_t = jnp.tile(jnp.concatenate([-sin, sin], axis=-1), (1, reps))
    return cos_t, sin_t


def kernel(x, ffn1_norm, ffn1_w_gate, ffn1_w_up, ffn1_w_down, mix_norm, w_in, swa_q_norm, swa_k_norm, swa_sinks, gla_w_gate, gla_gate_bias, gla_out_norm, w_proj_a, w_proj_b, w_out, ffn2_norm, ffn2_w_gate, ffn2_w_up, ffn2_w_down):
    batch, seq, d = x.shape
    assert d == D_MODEL and seq % INPROJ_TILE == 0 and (batch * seq) % FFN_TILE == 0
    n = batch * seq
    cos_t, sin_t = _rope_tables(seq)

    bf = lambda w: w.astype(BF16)
    wgate_p = jnp.concatenate(
        [gla_w_gate, jnp.zeros((DEPTH, GLR_PAD - GLA_GATE_RANK, GLA_K_W), F32)], axis=1).astype(BF16)
    head_id = jnp.arange(SWA_Q_W, dtype=jnp.int32) // SWA_HEAD_DIM
    head_ones = (head_id[:, None] == head_id[None, :]).astype(BF16)
    reps = LANES // SWA_HEAD_DIM
    q_gain = jnp.tile(swa_q_norm, (1, reps))
    k_gain = jnp.tile(swa_k_norm, (1, reps))
    row = lambda g, l: g[l:l + 1]

    w_in_t = jnp.transpose(w_in, (0, 2, 1))
    ffn1_w = (ffn1_w_gate, ffn1_w_up, ffn1_w_down)
    ffn2_w = (ffn2_w_gate, ffn2_w_up, ffn2_w_down)
    mix_w = (w_proj_a, w_proj_b, w_out)
    at = lambda ws, l: [(w, l) for w in ws]

    f1 = tuple(bf(w[0]) for w in ffn1_w)
    h = x.reshape(n, D_MODEL)
    for l in range(DEPTH):
        convs = [(w_in_t, l)] + at(ffn2_w, l) + (at(mix_w, 0) if l == 0 else [])
        h, conv = _ffn(h, row(ffn1_norm, l), *f1, convs=convs)
        w_t, f2 = conv[0], conv[1:4]
        if l == 0:
            wa, wb, wo = conv[4:]
        proj_args = (h, row(mix_norm, l), w_t, cos_t, sin_t, row(q_gain, l), row(k_gain, l), wgate_p[l],
                     row(gla_gate_bias, l), head_ones)
        qa, ka, va, qin, kin, vb, rb, ga, gb, fac = _inproj(*proj_args, seq)
        tail = (h, rb, ga, gb, row(gla_out_norm, l), wa, wb, wo)

        def exact_path(*ops):
            qraw, kraw, bcum = _inproj(*ops[16:], seq, raw=True)[-3:]
            ob = _gla_exact(qraw, kraw, ops[6], bcum, batch, seq)
            return _outproj(ops[8], _swa(*ops[:4], batch, seq), ob, *ops[9:16])

        h = lax.cond(
            jnp.min(fac[:, :CHUNKS_PER_TILE]) >= GLA_MIN_CHUNK_DECAY,
            lambda *ops: _mixer(*ops[:16], batch, seq), exact_path,
            qa, ka, va, swa_sinks[l], qin, kin, vb, fac, *tail, *proj_args)
        convs = [] if l + 1 == DEPTH else at(ffn1_w, l + 1) + at(mix_w, l + 1)
        h, conv = _ffn(h, row(ffn2_norm, l), *f2, convs=convs)
        if convs:
            f1, (wa, wb, wo) = conv[:3], conv[3:]
    return h.reshape(batch, seq, D_MODEL)
```

```python
import jax
import jax.numpy as jnp
from jax import lax
from jax.experimental import pallas as pl
from jax.experimental.pallas import tpu as pltpu

F32 = jnp.float32
BF16 = jnp.bfloat16

D_MODEL = 1024
DEPTH = 4
D_FF = 2816
EPS = 1e-6

SWA_HEADS = 8
SWA_KV_HEADS = 2
SWA_GROUP = SWA_HEADS // SWA_KV_HEADS
SWA_HEAD_DIM = 64
SWA_BLOCK = 128
ROPE_THETA = 10000.0
SWA_Q_W = SWA_HEADS * SWA_HEAD_DIM
SWA_KV_W = SWA_KV_HEADS * SWA_HEAD_DIM

GLA_HEADS = 4
GLA_DK = 128
GLA_DV = 256
GLA_GATE_RANK = 16
GLA_TAU = 16.0
GLA_K_W = GLA_HEADS * GLA_DK
GLA_V_W = GLA_HEADS * GLA_DV

IN_SPLITS = (SWA_Q_W, SWA_KV_W, SWA_KV_W, GLA_K_W, GLA_K_W, GLA_V_W, GLA_V_W, GLA_GATE_RANK, D_MODEL, D_MODEL)

LANES = 128
GLR_PAD = LANES
IN_COLS = sum(IN_SPLITS)
IN_OFF = tuple(sum(IN_SPLITS[:j]) for j in range(len(IN_SPLITS)))
(P_QA, P_KA, P_VA, P_QB, P_KB, P_VB, P_RB, P_GLR, P_GA, P_GB) = range(len(IN_SPLITS))

ROW_TILE = 512
FFN_TILE = 1024
FFN_SUB = 256
GLA_CHUNK = 128
CHUNKS_PER_TILE = ROW_TILE // GLA_CHUNK
GLA_FACTOR_ROWS = 3 * CHUNKS_PER_TILE
VMEM_LIMIT = 56 * 1024 * 1024
GLA_MIN_CHUNK_DECAY = 8.7e-27


def _rms(x, gain):
    ms = jnp.mean(x * x, axis=-1, keepdims=True)
    return x * lax.rsqrt(ms + EPS) * gain


def _sigmoid(x):
    return 0.5 * jnp.tanh(0.5 * x) + 0.5


def _dot(a, b):
    return jnp.dot(a, b, preferred_element_type=F32)


def _dot_nt(a, b):
    return lax.dot_general(a, b, (((1,), (1,)), ((), ())), preferred_element_type=F32)


def _dot_tn(a, b):
    return lax.dot_general(a, b, (((0,), (0,)), ((), ())), preferred_element_type=F32)


def _hi_lo(x):
    hi = x.astype(BF16)
    return hi, (x - hi.astype(F32)).astype(BF16)


def _resident(shape, index_map):
    return pl.BlockSpec(shape, index_map, pipeline_mode=pl.Buffered(1))


def _params(n_axes):
    return pltpu.CompilerParams(dimension_semantics=("arbitrary",) * n_axes,
                                vmem_limit_bytes=VMEM_LIMIT)


BF16_SUBLANES = 16


def _dense_call(body, name, n, in_specs, args, out_specs, out_shapes, convs=None, tile=None):
    nsteps = n // (tile or ROW_TILE)
    n_in, n_out, n_conv = len(in_specs), len(out_specs), len(convs or ())
    in_specs, out_specs, out_shapes, args = list(in_specs), list(out_specs), list(out_shapes), list(args)
    for w, layer in convs or ():
        _, rows, cols = w.shape
        nblk = max(b for b in range(1, nsteps + 1) if rows % (b * BF16_SUBLANES) == 0)
        last = nblk - 1
        in_specs.append(pl.BlockSpec((None, rows // nblk, cols),
                                     lambda i, layer=layer, last=last: (layer, jnp.minimum(i, last), 0)))
        out_specs.append(pl.BlockSpec((rows // nblk, cols), lambda i, last=last: (jnp.minimum(i, last), 0)))
        out_shapes.append(jax.ShapeDtypeStruct((rows, cols), BF16))
        args.append(w)

    def wrapped(*refs):
        ins, srcs = refs[:n_in], refs[n_in:n_in + n_conv]
        outs, dsts = refs[n_in + n_conv:n_in + n_conv + n_out], refs[n_in + n_conv + n_out:]

        def cast_weights():
            for s, d in zip(srcs, dsts):
                d[...] = s[...].astype(BF16)

        if convs is None:
            body(*ins, *outs)
        else:
            body(cast_weights, *ins, *outs)

    res = pl.pallas_call(
        wrapped, grid=(nsteps,), in_specs=in_specs, out_specs=out_specs, out_shape=out_shapes,
        compiler_params=_params(1), name=name)(*args)
    return res[:n_out], res[n_out:]


def _whole(shape):
    return _resident(shape, lambda i: (0,) * len(shape))


def _ffn_body(cast_weights, x_ref, gain_ref, wg_ref, wu_ref, wd_ref, o_ref):
    nsub = FFN_TILE // FFN_SUB
    rows = [slice(j * FFN_SUB, (j + 1) * FFN_SUB) for j in range(nsub)]

    def up(j):
        h = _rms(x_ref[rows[j], :], gain_ref[...]).astype(BF16)
        return _dot(h, wg_ref[...]), _dot(h, wu_ref[...])

    def down(j, g, u):
        a = (g * _sigmoid(g) * u).astype(BF16)
        o_ref[rows[j], :] = x_ref[rows[j], :] + 0.5 * _dot(a, wd_ref[...])

    pending = up(0)
    for j in range(nsub):
        nxt = up(j + 1) if j + 1 < nsub else None
        if j == 0:
            cast_weights()
        down(j, *pending)
        pending = nxt


def _ffn(x, gain, wg, wu, wd, convs=()):
    n = x.shape[0]
    row = pl.BlockSpec((FFN_TILE, D_MODEL), lambda i: (i, 0))
    (out,), converted = _dense_call(
        _ffn_body, "ffn", n,
        [row, _whole((1, D_MODEL)), _whole((D_MODEL, D_FF)), _whole((D_MODEL, D_FF)), _whole((D_FF, D_MODEL))],
        (x, gain, wg, wu, wd), [row], [jax.ShapeDtypeStruct((n, D_MODEL), F32)], convs, tile=FFN_TILE)
    return out, converted


def _head_norm_rope(x, ssq, gain, cos, sin_signed, lane):
    xn = x * lax.rsqrt(ssq * (1.0 / SWA_HEAD_DIM) + EPS) * gain
    half = SWA_HEAD_DIM // 2
    partner = jnp.where((lane % SWA_HEAD_DIM) < half,
                        pltpu.roll(xn, LANES - half, 1), pltpu.roll(xn, half, 1))
    return xn * cos + partner * sin_signed


def _inproj_body(x_ref, gain_ref, w_ref, cos_ref, sin_ref, qg_ref, kg_ref, wgate_ref, bias_ref,
                 ones_ref, qa_ref, ka_ref, va_ref, qin_ref, kin_ref, vb_ref, rb_ref,
                 ga_ref, gb_ref, fac_ref, *raw_refs):
    h = _rms(x_ref[...], gain_ref[...]).astype(BF16)
    half = D_MODEL // 2
    lane = lax.broadcasted_iota(jnp.int32, (1, LANES), 1)

    def proj(which, lo=0, width=None):
        off = IN_OFF[which] + lo
        width = IN_SPLITS[which] if width is None else width
        return _dot_nt(h, w_ref[off:off + width, :])

    narrow = proj(P_QA, 0, IN_OFF[P_QB])
    tail = proj(P_RB, half, half + LANES)
    glr = jnp.where(lane < GLA_GATE_RANK, tail[:, half:], 0.0).astype(BF16)
    logits = _dot(glr, wgate_ref[...]) + bias_ref[...]
    r = tail[:, :half]
    rb_ref[:, half:] = (r * _sigmoid(r)).astype(BF16)
    q = proj(P_QB) * GLA_DK ** -0.5
    la = (jnp.minimum(logits, 0.0) - jnp.log1p(jnp.exp(-jnp.abs(logits)))) * (1.0 / GLA_TAU)
    la_hi, la_lo = _hi_lo(la)
    k = proj(P_KB)
    if raw_refs:
        raw_refs[0][...] = q.astype(BF16)
        raw_refs[1][...] = k.astype(BF16)

    c = GLA_CHUNK
    t_i = lax.broadcasted_iota(jnp.int32, (c, c), 0)
    s_i = lax.broadcasted_iota(jnp.int32, (c, c), 1)
    tril = (s_i <= t_i).astype(BF16)
    cums = []
    for ci in range(CHUNKS_PER_TILE):
        rows = slice(ci * c, (ci + 1) * c)
        cums.append(_dot(tril, la_hi[rows]) + _dot(tril, la_lo[rows]))

    def decay_products(ci):
        rows = slice(ci * c, (ci + 1) * c)
        b = cums[ci]
        b_last = b[c - 1:c, :]
        b_mid = b[c // 2 - 1:c // 2, :]
        qin_ref[rows, :] = (q[rows] * jnp.exp(b - b_mid)).astype(BF16)
        kin_ref[rows, :] = (k[rows] * jnp.exp(b_mid - b)).astype(BF16)
        fac_ref[0, ci:ci + 1, :] = jnp.exp(b_last)
        fac_ref[0, CHUNKS_PER_TILE + ci:CHUNKS_PER_TILE + ci + 1, :] = jnp.exp(b_mid)
        fac_ref[0, 2 * CHUNKS_PER_TILE + ci:2 * CHUNKS_PER_TILE + ci + 1, :] = jnp.exp(b_last - b_mid)
        if raw_refs:
            raw_refs[2][rows, :] = b

    cos, sin = cos_ref[...], sin_ref[...]
    qa = narrow[:, :SWA_Q_W]
    ssq_q = _dot((qa * qa).astype(BF16), ones_ref[...])

    def rope_q(col):
        cs = slice(col * LANES, (col + 1) * LANES)
        qc = _head_norm_rope(qa[:, cs], ssq_q[:, cs], qg_ref[...], cos, sin, lane)
        qa_ref[:, cs] = (qc * SWA_HEAD_DIM ** -0.5).astype(BF16)

    vb_ref[:, :half] = proj(P_VB, 0, half).astype(BF16)
    decay_products(0)
    r = proj(P_RB, 0, half)
    decay_products(1)
    rb_ref[:, :half] = (r * _sigmoid(r)).astype(BF16)
    decay_products(2)
    for col, (which, ref, lo) in enumerate(((P_GA, ga_ref, 0), (P_GA, ga_ref, half),
                                             (P_GB, gb_ref, 0), (P_GB, gb_ref, half))):
        ref[:, lo:lo + half] = _sigmoid(proj(which, lo, half)).astype(BF16)
        if col == 0:
            decay_products(3)
        rope_q(col)
    ka = narrow[:, IN_OFF[P_KA]:IN_OFF[P_VA]]
    ssq_k = _dot((ka * ka).astype(BF16), ones_ref[0:SWA_KV_W, 0:SWA_KV_W])
    ka_ref[...] = _head_norm_rope(ka, ssq_k, kg_ref[...], cos, sin, lane).astype(BF16)
    va_ref[...] = narrow[:, IN_OFF[P_VA]:IN_OFF[P_QB]].astype(BF16)
    vb_ref[:, half:] = proj(P_VB, half, half).astype(BF16)


def _inproj(x, gain, w, cos, sin, q_gain, k_gain, wgate, bias, head_ones, seq, raw=False):
    n = x.shape[0]
    nt = n // ROW_TILE
    tiles_per_seq = seq // ROW_TILE
    row = lambda i: (i, 0)
    pos = lambda i: (i % tiles_per_seq, 0)
    widths = (SWA_Q_W, SWA_KV_W, SWA_KV_W, GLA_K_W, GLA_K_W, GLA_V_W, GLA_V_W, D_MODEL, D_MODEL)
    raw_types = (BF16, BF16, F32) if raw else ()
    outs, _ = _dense_call(
        _inproj_body, "inproj", n,
        [pl.BlockSpec((ROW_TILE, D_MODEL), row), _whole((1, D_MODEL)), _whole((IN_COLS, D_MODEL)),
         pl.BlockSpec((ROW_TILE, LANES), pos), pl.BlockSpec((ROW_TILE, LANES), pos),
         _whole((1, LANES)), _whole((1, LANES)), _whole((GLR_PAD, GLA_K_W)), _whole((1, GLA_K_W)),
         _whole((SWA_Q_W, SWA_Q_W))],
        (x, gain, w, cos, sin, q_gain, k_gain, wgate, bias, head_ones),
        [pl.BlockSpec((ROW_TILE, width), row) for width in widths]
        + [pl.BlockSpec((1, GLA_FACTOR_ROWS, GLA_K_W), lambda i: (i, 0, 0))]
        + [pl.BlockSpec((ROW_TILE, GLA_K_W), row) for _ in raw_types],
        [jax.ShapeDtypeStruct((n, width), BF16) for width in widths]
        + [jax.ShapeDtypeStruct((nt, GLA_FACTOR_ROWS, GLA_K_W), F32)]
        + [jax.ShapeDtypeStruct((n, GLA_K_W), dtype) for dtype in raw_types])
    return outs


def _swa_setup(i, k_ref, v_ref, kbuf, vbuf):
    lane = lax.broadcasted_iota(jnp.int32, (1, LANES), 1)
    first_half = lane < SWA_HEAD_DIM

    @pl.when(i == 0)
    def _():
        kbuf[0:SWA_BLOCK, :] = jnp.zeros((SWA_BLOCK, LANES), BF16)
        vbuf[0:SWA_BLOCK, :] = jnp.zeros((SWA_BLOCK, LANES), BF16)

    kbuf[SWA_BLOCK:, :] = k_ref[...]
    vbuf[SWA_BLOCK:, :] = v_ref[...]
    kall = kbuf[...]
    vall = vbuf[...]
    kswap = pltpu.roll(kall, SWA_HEAD_DIM, 1)
    vswap = pltpu.roll(vall, SWA_HEAD_DIM, 1)
    kdup = [jnp.where(first_half, kall, kswap), jnp.where(first_half, kswap, kall)]
    vdup = [jnp.where(first_half, vall, vswap), jnp.where(first_half, vswap, vall)]

    rows = SWA_GROUP * SWA_BLOCK
    t_idx = lax.broadcasted_iota(jnp.int32, (rows, SWA_BLOCK), 0) % SWA_BLOCK
    c_idx = lax.broadcasted_iota(jnp.int32, (rows, SWA_BLOCK), 1)
    use_cur = c_idx <= t_idx
    row_head = lax.broadcasted_iota(jnp.int32, (rows, 1), 0) // SWA_BLOCK
    return first_half, kdup, vdup, use_cur, row_head


def _swa_scores(ctx, q_ref, kvh, blk):
    first_half, kdup = ctx[0], ctx[1]
    r0 = blk * SWA_BLOCK
    parts = []
    for g in range(SWA_GROUP):
        head = kvh * SWA_GROUP + g
        qc = q_ref[r0:r0 + SWA_BLOCK, (head // 2) * LANES:(head // 2 + 1) * LANES]
        keep = first_half if head % 2 == 0 else jnp.logical_not(first_half)
        parts.append(jnp.where(keep, qc, jnp.zeros_like(qc)))
    return _dot_nt(jnp.concatenate(parts, axis=0), kdup[kvh][r0:r0 + 2 * SWA_BLOCK, :])


def _swa_attend(ctx, i, sinks_ref, o_ref, s2, kvh, blk):
    first_half, _, vdup, use_cur, row_head = ctx
    rows = SWA_GROUP * SWA_BLOCK
    sink = jnp.zeros((rows, 1), F32)
    for g in range(SWA_GROUP):
        sink = jnp.where(row_head == g, sinks_ref[kvh * SWA_GROUP + g], sink)
    r0 = blk * SWA_BLOCK
    s_prev = s2[:, :SWA_BLOCK]
    if blk == 0:
        s_prev = s_prev + jnp.where(i > 0, 0.0, -jnp.inf)
    s = jnp.where(use_cur, s2[:, SWA_BLOCK:], s_prev)
    m = jnp.maximum(jnp.max(s, axis=-1, keepdims=True), sink)
    p = jnp.exp(s - m)
    denom = jnp.sum(p, axis=-1, keepdims=True) + jnp.exp(sink - m)
    p2 = jnp.concatenate([jnp.where(use_cur, 0.0, p), jnp.where(use_cur, p, 0.0)], axis=1)
    vals = vdup[kvh][r0:r0 + 2 * SWA_BLOCK, :]
    o = _dot(p2.astype(BF16), vals) / denom
    for pair in range(SWA_GROUP // 2):
        a = o[(2 * pair) * SWA_BLOCK:(2 * pair + 1) * SWA_BLOCK, :]
        b = o[(2 * pair + 1) * SWA_BLOCK:(2 * pair + 2) * SWA_BLOCK, :]
        col = (kvh * SWA_GROUP) // 2 + pair
        o_ref[r0:r0 + SWA_BLOCK, col * LANES:(col + 1) * LANES] = (
            jnp.where(first_half, a, b).astype(o_ref.dtype))


def _swa_carry(kbuf, vbuf):
    kbuf[0:SWA_BLOCK, :] = kbuf[ROW_TILE:ROW_TILE + SWA_BLOCK, :]
    vbuf[0:SWA_BLOCK, :] = vbuf[ROW_TILE:ROW_TILE + SWA_BLOCK, :]


def _decay_columns(decay_row):
    cols = jnp.transpose(jnp.broadcast_to(decay_row, (GLA_CHUNK, GLA_DK)))
    return jnp.concatenate([cols] * (GLA_DV // GLA_CHUNK), axis=1)


def _gla_tile(i, qin_ref, kin_ref, v_ref, fac_ref, o_ref, s_ref):
    c = GLA_CHUNK

    @pl.when(i == 0)
    def _():
        s_ref[...] = jnp.zeros(s_ref.shape, F32)

    t_i = lax.broadcasted_iota(jnp.int32, (c, c), 0)
    s_i = lax.broadcasted_iota(jnp.int32, (c, c), 1)
    causal = s_i <= t_i
    items = [(ci, h) for ci in range(CHUNKS_PER_TILE) for h in range(GLA_HEADS)]
    window = lambda ci, h: (slice(ci * c, (ci + 1) * c), slice(h * GLA_DK, (h + 1) * GLA_DK),
                            slice(h * GLA_DV, (h + 1) * GLA_DV))
    factor = lambda kind, ci, ks: fac_ref[0, kind * CHUNKS_PER_TILE + ci:kind * CHUNKS_PER_TILE + ci + 1, ks]

    attn, update = {}, {}
    for ci, h in items:
        rows, ks, vs = window(ci, h)
        scores = _dot_nt(qin_ref[rows, ks], kin_ref[rows, ks])
        attn[ci, h] = jnp.where(causal, scores, 0.0).astype(BF16)
        k_state = (kin_ref[rows, ks].astype(F32) * factor(2, ci, ks)).astype(BF16)
        update[ci, h] = _dot_tn(k_state, v_ref[rows, vs])

    states = [s_ref[h] for h in range(GLA_HEADS)]
    for ci, h in items:
        rows, ks, vs = window(ci, h)
        q_decayed = (qin_ref[rows, ks].astype(F32) * factor(1, ci, ks)).astype(BF16)
        lhs = jnp.concatenate([q_decayed, attn[ci, h]], axis=1)
        rhs = jnp.concatenate([states[h].astype(BF16), v_ref[rows, vs]], axis=0)
        o_ref[rows, vs] = _dot(lhs, rhs).astype(o_ref.dtype)
        states[h] = _decay_columns(factor(0, ci, ks)) * states[h] + update[ci, h]
    for h in range(GLA_HEADS):
        s_ref[h] = states[h]


def _swa_tile(i, sinks_ref, q_ref, k_ref, v_ref, o_ref, kbuf, vbuf):
    ctx = _swa_setup(i, k_ref, v_ref, kbuf, vbuf)
    items = [(kvh, blk) for kvh in range(SWA_KV_HEADS) for blk in range(ROW_TILE // SWA_BLOCK)]
    scores = [_swa_scores(ctx, q_ref, kvh, blk) for kvh, blk in items]
    for s2, (kvh, blk) in zip(scores, items):
        _swa_attend(ctx, i, sinks_ref, o_ref, s2, kvh, blk)
    _swa_carry(kbuf, vbuf)


def _mixer_body(sinks_ref, qa_ref, ka_ref, va_ref, qin_ref, kin_ref, vb_ref, fac_ref,
                x_ref, rb_ref, ga_ref, gb_ref, og_ref, wa_ref, wb_ref, wo_ref,
                o_ref, kbuf, vbuf, s_ref, oa_scr, ob_scr):
    i = pl.program_id(1)
    quarter = D_MODEL // GLA_HEADS
    _gla_tile(i, qin_ref, kin_ref, vb_ref, fac_ref, ob_scr, s_ref)

    ctx = _swa_setup(i, ka_ref, va_ref, kbuf, vbuf)
    items = [(kvh, blk) for kvh in range(SWA_KV_HEADS) for blk in range(ROW_TILE // SWA_BLOCK)]
    scores = [_swa_scores(ctx, qa_ref, kvh, blk) for kvh, blk in items]
    gated = []
    for h in range(GLA_HEADS):
        vs = slice(h * GLA_DV, (h + 1) * GLA_DV)
        on = _rms(ob_scr[:, vs].astype(F32), og_ref[:, vs])
        gated.append((on * rb_ref[:, vs].astype(F32)).astype(BF16))
    gated = jnp.concatenate(gated, axis=1)
    per_col = len(items) // GLA_HEADS
    yb = []
    for j in range(GLA_HEADS):
        yb.append(_dot(gated, wb_ref[:, j * quarter:(j + 1) * quarter]))
        for n in range(j * per_col, (j + 1) * per_col):
            _swa_attend(ctx, i, sinks_ref, oa_scr, scores[n], *items[n])
    _swa_carry(kbuf, vbuf)

    merged = []
    for j in range(GLA_HEADS):
        cs = slice(j * quarter, (j + 1) * quarter)
        ya = _dot(oa_scr[...], wa_ref[:, cs])
        merged.append((ga_ref[:, cs].astype(F32) * ya + gb_ref[:, cs].astype(F32) * yb[j]).astype(BF16))
    o_ref[...] = x_ref[...] + _dot(jnp.concatenate(merged, axis=1), wo_ref[...])


def _swa_scratch():
    return [pltpu.VMEM((ROW_TILE + SWA_BLOCK, LANES), BF16), pltpu.VMEM((ROW_TILE + SWA_BLOCK, LANES), BF16)]


def _mixer(qa, ka, va, sinks, qin, kin, vb, fac, x, rb, ga, gb, out_gain, wa, wb, wo, batch, seq):
    nt = seq // ROW_TILE
    tok = lambda b, i, *_: (b * nt + i, 0)
    spec = lambda width: pl.BlockSpec((ROW_TILE, width), tok)
    fixed = lambda shape: _resident(shape, lambda b, i, *_: (0,) * len(shape))
    return pl.pallas_call(
        _mixer_body,
        grid_spec=pltpu.PrefetchScalarGridSpec(
            num_scalar_prefetch=1,
            grid=(batch, nt),
            in_specs=[spec(SWA_Q_W), spec(SWA_KV_W), spec(SWA_KV_W),
                      spec(GLA_K_W), spec(GLA_K_W), spec(GLA_V_W),
                      pl.BlockSpec((1, GLA_FACTOR_ROWS, GLA_K_W), lambda b, i, *_: (b * nt + i, 0, 0)),
                      spec(D_MODEL), spec(GLA_V_W), spec(D_MODEL), spec(D_MODEL),
                      fixed((1, GLA_V_W)), fixed((SWA_Q_W, D_MODEL)), fixed((GLA_V_W, D_MODEL)),
                      fixed((D_MODEL, D_MODEL))],
            out_specs=spec(D_MODEL),
            scratch_shapes=_swa_scratch() + [pltpu.VMEM((GLA_HEADS, GLA_DK, GLA_DV), F32),
                                             pltpu.VMEM((ROW_TILE, SWA_Q_W), BF16),
                                             pltpu.VMEM((ROW_TILE, GLA_V_W), BF16)],
        ),
        out_shape=jax.ShapeDtypeStruct((batch * seq, D_MODEL), F32),
        compiler_params=_params(2),
        name="mixer",
    )(sinks, qa, ka, va, qin, kin, vb, fac, x, rb, ga, gb, out_gain, wa, wb, wo)


def _swa_body(sinks_ref, q_ref, k_ref, v_ref, o_ref, kbuf, vbuf):
    _swa_tile(pl.program_id(1), sinks_ref, q_ref, k_ref, v_ref, o_ref, kbuf, vbuf)


def _swa(qa, ka, va, sinks, batch, seq):
    nt = seq // ROW_TILE
    tok = lambda b, i, *_: (b * nt + i, 0)
    return pl.pallas_call(
        _swa_body,
        grid_spec=pltpu.PrefetchScalarGridSpec(
            num_scalar_prefetch=1,
            grid=(batch, nt),
            in_specs=[
                pl.BlockSpec((ROW_TILE, SWA_Q_W), tok),
                pl.BlockSpec((ROW_TILE, SWA_KV_W), tok),
                pl.BlockSpec((ROW_TILE, SWA_KV_W), tok),
            ],
            out_specs=pl.BlockSpec((ROW_TILE, SWA_Q_W), tok),
            scratch_shapes=_swa_scratch(),
        ),
        out_shape=jax.ShapeDtypeStruct((batch * seq, SWA_Q_W), BF16),
        compiler_params=_params(2),
        name="swa",
    )(sinks, qa, ka, va)


def _gla_exact_body(q_ref, k_ref, v_ref, b_ref, o_ref, s_ref, kf_ref, bf_ref):
    i = pl.program_id(1)
    c = GLA_CHUNK

    @pl.when(i == 0)
    def _():
        s_ref[...] = jnp.zeros(s_ref.shape, F32)

    t_col = lax.broadcasted_iota(jnp.int32, (c, 1), 0)
    s_lane = lax.broadcasted_iota(jnp.int32, (1, c), 1)

    for h in range(GLA_HEADS):
        ks = slice(h * GLA_DK, (h + 1) * GLA_DK)
        vs = slice(h * GLA_DV, (h + 1) * GLA_DV)

        def chunk_step(ci, carry, h=h, ks=ks, vs=vs):
            rows = pl.ds(pl.multiple_of(ci * c, c), c)
            b = b_ref[rows, ks]
            q = q_ref[rows, ks].astype(F32)
            k = k_ref[rows, ks].astype(F32)
            v = v_ref[rows, vs]
            kf_ref[...] = k
            bf_ref[...] = b

            def key_column(s, attn):
                pair = jnp.exp(jnp.minimum(b - bf_ref[pl.ds(s, 1), :], 0.0))
                col = jnp.sum(q * pair * kf_ref[pl.ds(s, 1), :], axis=-1, keepdims=True)
                col = jnp.where(t_col >= s, col, 0.0)
                return attn + col * (s_lane == s).astype(F32)

            attn = lax.fori_loop(0, c, key_column, jnp.zeros((c, c), F32))
            b_last = b[c - 1:c, :]
            state = s_ref[h]
            o = _dot((q * jnp.exp(b)).astype(BF16), state.astype(BF16)) + _dot(attn.astype(BF16), v)
            o_ref[rows, vs] = o.astype(o_ref.dtype)
            s_ref[h] = (_decay_columns(jnp.exp(b_last)) * state
                        + _dot_tn((k * jnp.exp(b_last - b)).astype(BF16), v))
            return carry

        lax.fori_loop(0, CHUNKS_PER_TILE, chunk_step, 0)


def _gla_exact(q, k, vb, bcum, batch, seq):
    nt = seq // ROW_TILE
    tok = lambda b, i: (b * nt + i, 0)
    kspec = pl.BlockSpec((ROW_TILE, GLA_K_W), tok)
    vspec = pl.BlockSpec((ROW_TILE, GLA_V_W), tok)
    return pl.pallas_call(
        _gla_exact_body,
        grid=(batch, nt),
        in_specs=[kspec, kspec, vspec, kspec],
        out_specs=vspec,
        out_shape=jax.ShapeDtypeStruct((batch * seq, GLA_V_W), BF16),
        scratch_shapes=[pltpu.VMEM((GLA_HEADS, GLA_DK, GLA_DV), F32),
                        pltpu.VMEM((GLA_CHUNK, GLA_DK), F32), pltpu.VMEM((GLA_CHUNK, GLA_DK), F32)],
        compiler_params=_params(2),
        name="gla_exact",
    )(q, k, vb, bcum)


def _outproj_body(x_ref, oa_ref, ob_ref, rb_ref, ga_ref, gb_ref, og_ref, wa_ref, wb_ref, wo_ref, o_ref):
    quarter = D_MODEL // GLA_HEADS
    ya = []
    gated = []
    for h in range(GLA_HEADS):
        vs = slice(h * GLA_DV, (h + 1) * GLA_DV)
        ya.append(_dot(oa_ref[...], wa_ref[:, h * quarter:(h + 1) * quarter]))
        on = _rms(ob_ref[:, vs].astype(F32), og_ref[:, vs])
        gated.append((on * rb_ref[:, vs].astype(F32)).astype(BF16))
    gated = jnp.concatenate(gated, axis=1)
    merged = []
    for j in range(GLA_HEADS):
        cs = slice(j * quarter, (j + 1) * quarter)
        yb = _dot(gated, wb_ref[:, cs])
        merged.append((ga_ref[:, cs].astype(F32) * ya[j] + gb_ref[:, cs].astype(F32) * yb).astype(BF16))
    o_ref[...] = x_ref[...] + _dot(jnp.concatenate(merged, axis=1), wo_ref[...])


def _outproj(x, oa, ob, rb, ga, gb, out_gain, wa, wb, wo):
    n = x.shape[0]
    spec = lambda width: pl.BlockSpec((ROW_TILE, width), lambda i: (i, 0))
    (out,), _ = _dense_call(
        _outproj_body, "outproj", n,
        [spec(D_MODEL), spec(SWA_Q_W), spec(GLA_V_W), spec(GLA_V_W), spec(D_MODEL), spec(D_MODEL),
         _whole((1, GLA_V_W)), _whole((SWA_Q_W, D_MODEL)), _whole((GLA_V_W, D_MODEL)),
         _whole((D_MODEL, D_MODEL))],
        (x, oa, ob, rb, ga, gb, out_gain, wa, wb, wo),
        [spec(D_MODEL)], [jax.ShapeDtypeStruct((n, D_MODEL), F32)])
    return out


def _rope_tables(seq):
    inv_freq = ROPE_THETA ** (-jnp.arange(0, SWA_HEAD_DIM, 2, dtype=F32) / SWA_HEAD_DIM)
    ang = jnp.arange(seq, dtype=F32)[:, None] * inv_freq[None, :]
    cos, sin = jnp.cos(ang), jnp.sin(ang)
    reps = LANES // SWA_HEAD_DIM
    cos_t = jnp.tile(jnp.concatenate([cos, cos], axis=-1), (1, reps))
    sin_t = jnp.tile(jnp.concatenate([-sin, sin], axis=-1), (1, reps))
    return cos_t, sin_t


def kernel(x, ffn1_norm, ffn1_w_gate, ffn1_w_up, ffn1_w_down, mix_norm, w_in, swa_q_norm, swa_k_norm, swa_sinks, gla_w_gate, gla_gate_bias, gla_out_norm, w_proj_a, w_proj_b, w_out, ffn2_norm, ffn2_w_gate, ffn2_w_up, ffn2_w_down):
    batch, seq, d = x.shape
    assert d == D_MODEL and seq % ROW_TILE == 0 and (batch * seq) % FFN_TILE == 0
    n = batch * seq
    cos_t, sin_t = _rope_tables(seq)

    bf = lambda w: w.astype(BF16)
    wgate_p = jnp.concatenate(
        [gla_w_gate, jnp.zeros((DEPTH, GLR_PAD - GLA_GATE_RANK, GLA_K_W), F32)], axis=1).astype(BF16)
    head_id = jnp.arange(SWA_Q_W, dtype=jnp.int32) // SWA_HEAD_DIM
    head_ones = (head_id[:, None] == head_id[None, :]).astype(BF16)
    reps = LANES // SWA_HEAD_DIM
    q_gain = jnp.tile(swa_q_norm, (1, reps))
    k_gain = jnp.tile(swa_k_norm, (1, reps))
    row = lambda g, l: g[l:l + 1]

    w_in_t = jnp.transpose(w_in, (0, 2, 1))
    ffn1_w = (ffn1_w_gate, ffn1_w_up, ffn1_w_down)
    ffn2_w = (ffn2_w_gate, ffn2_w_up, ffn2_w_down)
    mix_w = (w_proj_a, w_proj_b, w_out)
    at = lambda ws, l: [(w, l) for w in ws]

    f1 = tuple(bf(w[0]) for w in ffn1_w)
    h = x.reshape(n, D_MODEL)
    for l in range(DEPTH):
        convs = [(w_in_t, l)] + at(ffn2_w, l) + (at(mix_w, 0) if l == 0 else [])
        h, conv = _ffn(h, row(ffn1_norm, l), *f1, convs=convs)
        w_t, f2 = conv[0], conv[1:4]
        if l == 0:
            wa, wb, wo = conv[4:]
        proj_args = (h, row(mix_norm, l), w_t, cos_t, sin_t, row(q_gain, l), row(k_gain, l), wgate_p[l],
                     row(gla_gate_bias, l), head_ones)
        qa, ka, va, qin, kin, vb, rb, ga, gb, fac = _inproj(*proj_args, seq)
        tail = (h, rb, ga, gb, row(gla_out_norm, l), wa, wb, wo)

        def exact_path(*ops):
            qraw, kraw, bcum = _inproj(*ops[16:], seq, raw=True)[-3:]
            ob = _gla_exact(qraw, kraw, ops[6], bcum, batch, seq)
            return _outproj(ops[8], _swa(*ops[:4], batch, seq), ob, *ops[9:16])

        h = lax.cond(
            jnp.min(fac[:, :CHUNKS_PER_TILE]) >= GLA_MIN_CHUNK_DECAY,
            lambda *ops: _mixer(*ops[:16], batch, seq), exact_path,
            qa, ka, va, swa_sinks[l], qin, kin, vb, fac, *tail, *proj_args)
        convs = [] if l + 1 == DEPTH else at(ffn1_w, l + 1) + at(mix_w, l + 1)
        h, conv = _ffn(h, row(ffn2_norm, l), *f2, convs=convs)
        if convs:
            f1, (wa, wb, wo) = conv[:3], conv[3:]
    return h.reshape(batch, seq, D_MODEL)
```
